```python
import math
import jax, jax.numpy as jnp
from jax import lax
import numpy as np

D_MODEL = 2048
BATCH = 4
SEQ = 2048
DEPTH = 2
DEC_BATCH = 32
DEC_SEQ = 1
PAST_LEN = 16384
PAGE_SIZE = 128

HEAD_DIM = 128
NSA_HEADS = 8
NSA_KV = 2
NSA_REP = NSA_HEADS // NSA_KV
NSA_BLOCK = 64
NSA_N_SEL = 16
NSA_WINDOW = 512
NSA_FORCE = 1e4
FOX_HEADS = 8
SWA_HEADS = 32
SWA_KV = 4
SWA_REP = SWA_HEADS // SWA_KV
SWA_HEAD_DIM = 64
SWA_WINDOW = 128
Q_BLOCK = 128
SEL_Q_BLOCK = 64
D_FF = -(-8 * D_MODEL // (3 * 256)) * 256
PLE_DIM = 256
N_EVEN = (DEPTH + 1) // 2
N_ODD = DEPTH // 2
EPS = 1e-6
N_NORMS = 5

EVEN_SPLITS = [NSA_HEADS * HEAD_DIM, 6 * NSA_KV * HEAD_DIM, 3 * NSA_HEADS,
               FOX_HEADS * HEAD_DIM, FOX_HEADS * HEAD_DIM, FOX_HEADS * HEAD_DIM, FOX_HEADS]
EVEN_IN = sum(EVEN_SPLITS)
EVEN_OUT = NSA_HEADS * HEAD_DIM + FOX_HEADS * HEAD_DIM
ODD_SPLITS = [SWA_HEADS * SWA_HEAD_DIM, SWA_KV * SWA_HEAD_DIM, SWA_KV * SWA_HEAD_DIM]
ODD_IN = sum(ODD_SPLITS)
ODD_OUT = SWA_HEADS * SWA_HEAD_DIM

kernel_name = 'nsa_fox_swa_sink_hybrid_step'

F32 = jnp.float32


def rms_norm(x, g):
    xf = x.astype(F32)
    y = xf * lax.rsqrt(jnp.mean(xf * xf, axis=-1, keepdims=True) + EPS)
    return (y * g.astype(F32)).astype(x.dtype)


def alibi_slopes(n):
    return 2.0 ** (-8.0 * jnp.arange(1, n + 1, dtype=F32) / n)


def split_cols(z, sizes):
    return jnp.split(z, [int(o) for o in np.cumsum(sizes)[:-1]], axis=-1)


def masked_softmax(logits, mask, sink=None):
    l = jnp.where(mask, logits.astype(F32), -jnp.inf)
    m = jnp.max(l, axis=-1, keepdims=True)
    if sink is not None:
        m = jnp.maximum(m, sink)
    m = jnp.where(jnp.isfinite(m), m, 0.0)
    e = jnp.exp(l - m)
    den = jnp.sum(e, axis=-1, keepdims=True)
    if sink is not None:
        den = den + jnp.exp(sink - m)
    return e / jnp.maximum(den, 1e-30)


def local_attend(q, k, v, q_pos, k_pos, window, slopes, sink=None):
    scale = q.shape[-1] ** -0.5
    dist = q_pos[:, None] - k_pos[None, :]
    logits = jnp.einsum('bqgrd,bkgd->bgrqk', q, k).astype(F32) * scale
    logits = logits - slopes[None, :, :, None, None] * dist.astype(F32)
    mask = (dist >= 0) & (dist <= window) & (k_pos[None, :] >= 0)
    s = None if sink is None else sink.astype(F32)[None, :, :, None, None]
    p = masked_softmax(logits, mask, s)
    return jnp.einsum('bgrqk,bkgd->bqgrd', p.astype(v.dtype), v)


def banded_prompt(q, k, v, window, slopes, sink=None):
    s = q.shape[1]
    pad = ((0, 0), (window, 0), (0, 0), (0, 0))
    kp, vp = jnp.pad(k, pad), jnp.pad(v, pad)

    def one(i):
        st = i * Q_BLOCK
        q_i = lax.dynamic_slice_in_dim(q, st, Q_BLOCK, axis=1)
        k_i = lax.dynamic_slice_in_dim(kp, st, Q_BLOCK + window, axis=1)
        v_i = lax.dynamic_slice_in_dim(vp, st, Q_BLOCK + window, axis=1)
        q_pos = st + jnp.arange(Q_BLOCK)
        k_pos = st - window + jnp.arange(Q_BLOCK + window)
        return local_attend(q_i, k_i, v_i, q_pos, k_pos, window, slopes, sink)

    o = lax.map(one, jnp.arange(s // Q_BLOCK))
    return jnp.moveaxis(o, 0, 1).reshape(q.shape)


def nsa_compress(rows, pe, phi):
    b, t, g, d = rows.shape
    blk = rows.reshape(b, t // NSA_BLOCK, NSA_BLOCK, g, d)
    pooled = jnp.mean(blk + pe[None, None, :, None, :], axis=2)
    return jnp.einsum('bngd,de->bnge', pooled, phi)


def nsa_cmp_attend(q, kc, vc, q_pos, slopes):
    n = kc.shape[1]
    scale = q.shape[-1] ** -0.5
    blk_end = (jnp.arange(n) + 1) * NSA_BLOCK - 1
    centre = jnp.arange(n, dtype=F32) * NSA_BLOCK + (NSA_BLOCK - 1) / 2
    logits = jnp.einsum('bqgrd,bngd->bgrqn', q, kc).astype(F32) * scale
    logits = logits - slopes[None, :, :, None, None] * (q_pos.astype(F32)[:, None] - centre[None, :])
    mask = blk_end[None, :] <= q_pos[:, None]
    p = masked_softmax(logits, mask)
    out = jnp.einsum('bgrqn,bngd->bqgrd', p.astype(vc.dtype), vc)
    return out, jnp.sum(p, axis=2)


def nsa_select(imp, q_pos, n_blocks):
    imp = jnp.pad(imp, ((0, 0), (0, 0), (0, 0), (0, n_blocks - imp.shape[-1])))
    blk = jnp.arange(n_blocks)[None, :]
    cur = (q_pos // NSA_BLOCK)[:, None]
    forced = (blk == 0) | (blk == cur) | (blk == cur - 1)
    complete = (blk + 1) * NSA_BLOCK - 1 <= q_pos[:, None]
    score = jnp.where(forced, NSA_FORCE, jnp.where(complete, imp, -NSA_FORCE))
    return lax.top_k(score, min(NSA_N_SEL, n_blocks))[1]


def nsa_sel_attend(q, kg, vg, k_pos, q_pos, slopes):
    scale = q.shape[-1] ** -0.5
    dist = q_pos[None, None, :, None] - k_pos
    logits = jnp.einsum('bqgrd,bgqld->bgrql', q, kg).astype(F32) * scale
    logits = logits - slopes[None, :, :, None, None] * dist[:, :, None].astype(F32)
    p = masked_softmax(logits, (dist >= 0)[:, :, None])
    return jnp.einsum('bgrql,bgqld->bqgrd', p.astype(vg.dtype), vg)


def nsa_merge(gates, o_cmp, o_sel, o_win):
    return (gates[:, :, 0, ..., None] * o_cmp + gates[:, :, 1, ..., None] * o_sel
            + gates[:, :, 2, ..., None] * o_win)


def nsa_prompt(q, kv, gates, pe, phi, slopes):
    b, s = q.shape[:2]
    pos = jnp.arange(s)
    kc = nsa_compress(kv[:, :, 0], pe[0], phi[0])
    vc = nsa_compress(kv[:, :, 1], pe[1], phi[1])
    o_cmp, imp = nsa_cmp_attend(q, kc, vc, pos, slopes)
    n_blocks = s // NSA_BLOCK
    idx = nsa_select(imp, pos, n_blocks)
    n_sel = idx.shape[-1]
    kblk = kv[:, :, 2].reshape(b, n_blocks, NSA_BLOCK, NSA_KV, HEAD_DIM)
    vblk = kv[:, :, 3].reshape(b, n_blocks, NSA_BLOCK, NSA_KV, HEAD_DIM)
    bi = jnp.arange(b)[:, None, None, None]
    gi = jnp.arange(NSA_KV)[None, :, None, None]
    L = n_sel * NSA_BLOCK

    def one(c):
        st = c * SEL_Q_BLOCK
        q_c = lax.dynamic_slice_in_dim(q, st, SEL_Q_BLOCK, axis=1)
        idx_c = lax.dynamic_slice_in_dim(idx, st, SEL_Q_BLOCK, axis=2)
        kg = kblk[bi, idx_c, :, gi].reshape(b, NSA_KV, SEL_Q_BLOCK, L, HEAD_DIM)
        vg = vblk[bi, idx_c, :, gi].reshape(b, NSA_KV, SEL_Q_BLOCK, L, HEAD_DIM)
        k_pos = (idx_c[..., None] * NSA_BLOCK + jnp.arange(NSA_BLOCK)).reshape(b, NSA_KV, SEL_Q_BLOCK, L)
        return nsa_sel_attend(q_c, kg, vg, k_pos, st + jnp.arange(SEL_Q_BLOCK), slopes)

    o_sel = jnp.moveaxis(lax.map(one, jnp.arange(s // SEL_Q_BLOCK)), 0, 1).reshape(q.shape)
    o_win = banded_prompt(q, kv[:, :, 4], kv[:, :, 5], NSA_WINDOW, slopes)
    return nsa_merge(gates, o_cmp, o_sel, o_win)


def nsa_sample(q, kv, gates, cache_l, win_buf, page_table, pe, phi, slopes):
    db, tq = q.shape[:2]
    past = page_table.shape[1] * PAGE_SIZE
    q_pos = past + jnp.arange(tq)
    n_new_cmp = tq // NSA_BLOCK

    def past_rows(c):
        return cache_l[page_table, :, c].reshape(db, past, NSA_KV, HEAD_DIM)

    kc = jnp.concatenate([nsa_compress(past_rows(0), pe[0], phi[0]),
                          nsa_compress(kv[:, :n_new_cmp * NSA_BLOCK, 0], pe[0], phi[0])], axis=1)
    vc = jnp.concatenate([nsa_compress(past_rows(1), pe[1], phi[1]),
                          nsa_compress(kv[:, :n_new_cmp * NSA_BLOCK, 1], pe[1], phi[1])], axis=1)
    o_cmp, imp = nsa_cmp_attend(q, kc, vc, q_pos, slopes)

    n_past_blk = past // NSA_BLOCK
    n_new_blk = -(-tq // NSA_BLOCK)
    idx = nsa_select(imp, q_pos, n_past_blk + n_new_blk)
    n_sel = idx.shape[-1]
    L = n_sel * NSA_BLOCK
    blk_per_page = PAGE_SIZE // NSA_BLOCK
    bi = jnp.arange(db)[:, None, None, None]
    gi = jnp.arange(NSA_KV)[None, :, None, None]
    jp = jnp.minimum(idx, n_past_blk - 1)
    phys = page_table[bi, jp // blk_per_page]
    rows = (jp % blk_per_page)[..., None] * NSA_BLOCK + jnp.arange(NSA_BLOCK)
    jn = jnp.clip(idx - n_past_blk, 0, n_new_blk - 1)
    is_new = (idx >= n_past_blk)[..., None, None]
    pad = n_new_blk * NSA_BLOCK - tq

    def gather(c, new_rows):
        nb = jnp.pad(new_rows, ((0, 0), (0, pad), (0, 0), (0, 0))).reshape(db, n_new_blk, NSA_BLOCK, NSA_KV, HEAD_DIM)
        from_past = cache_l[phys[..., None], rows, c, gi[..., None]]
        from_new = nb[bi, jn, :, gi]
        return jnp.where(is_new, from_new, from_past).reshape(db, NSA_KV, tq, L, HEAD_DIM)

    kg = gather(2, kv[:, :, 2])
    vg = gather(3, kv[:, :, 3])
    k_pos = (idx[..., None] * NSA_BLOCK + jnp.arange(NSA_BLOCK)).reshape(db, NSA_KV, tq, L)
    o_sel = nsa_sel_attend(q, kg, vg, k_pos, q_pos, slopes)

    w = win_buf.shape[1]
    all_win = jnp.concatenate([win_buf, kv[:, :, 4:6]], axis=1)
    k_pos_w = past - w + jnp.arange(w + tq)
    o_win = local_attend(q, all_win[:, :, 0], all_win[:, :, 1], q_pos, k_pos_w, NSA_WINDOW, slopes)
    return nsa_merge(gates, o_cmp, o_sel, o_win), all_win[:, -w:]


def fox_logits(q, k, F_q, F_k, q_pos, k_pos):
    scale = q.shape[-1] ** -0.5
    l = jnp.einsum('bqhd,bkhd->bhqk', q, k).astype(F32) * scale
    l = l + jnp.swapaxes(F_q, 1, 2)[..., :, None] - jnp.swapaxes(F_k, 1, 2)[..., None, :]
    return l, k_pos[None, :] <= q_pos[:, None]


def fox_prompt(q, k, v, logf):
    s = q.shape[1]
    F = jnp.cumsum(logf, axis=1)
    k_pos = jnp.arange(s)

    def one(i):
        st = i * Q_BLOCK
        q_i = lax.dynamic_slice_in_dim(q, st, Q_BLOCK, axis=1)
        F_i = lax.dynamic_slice_in_dim(F, st, Q_BLOCK, axis=1)
        l, m = fox_logits(q_i, k, F_i, F, st + jnp.arange(Q_BLOCK), k_pos)
        p = masked_softmax(l, m)
        return jnp.einsum('bhqk,bkhd->bqhd', p.astype(v.dtype), v)

    o = lax.map(one, jnp.arange(s // Q_BLOCK))
    return jnp.moveaxis(o, 0, 1).reshape(q.shape)


def fox_sample(q, k_new, v_new, logf_new, cache_kv_l, cache_logf_l, page_table):
    db, tq = q.shape[:2]
    past = page_table.shape[1] * PAGE_SIZE
    k_past = cache_kv_l[page_table, :, 0].reshape(db, past, FOX_HEADS, HEAD_DIM)
    v_past = cache_kv_l[page_table, :, 1].reshape(db, past, FOX_HEADS, HEAD_DIM)
    logf_past = cache_logf_l[page_table].reshape(db, past, FOX_HEADS).astype(F32)
    F = jnp.cumsum(jnp.concatenate([logf_past, logf_new], axis=1), axis=1)
    F_past, F_new = F[:, :past], F[:, past:]
    q_pos = past + jnp.arange(tq)
    l_p, m_p = fox_logits(q, k_past, F_new, F_past, q_pos, jnp.arange(past))
    l_n, m_n = fox_logits(q, k_new, F_new, F_new, q_pos, q_pos)
    p = masked_softmax(jnp.concatenate([l_p, l_n], axis=-1), jnp.concatenate([m_p, m_n], axis=-1))
    return (jnp.einsum('bhqk,bkhd->bqhd', p[..., :past].astype(v_past.dtype), v_past)
            + jnp.einsum('bhqk,bkhd->bqhd', p[..., past:].astype(v_new.dtype), v_new))


def split_even(z, gate_b, forget_b):
    b, t = z.shape[:2]
    q_a, kv_a, g_a, q_b, k_b, v_b, f_b = split_cols(z, EVEN_SPLITS)
    q_a = q_a.reshape(b, t, NSA_KV, NSA_REP, HEAD_DIM)
    kv_a = kv_a.reshape(b, t, 6, NSA_KV, HEAD_DIM)
    gates = jax.nn.sigmoid(g_a + gate_b).reshape(b, t, 3, NSA_KV, NSA_REP)
    q_b = q_b.reshape(b, t, FOX_HEADS, HEAD_DIM)
    k_b = k_b.reshape(b, t, FOX_HEADS, HEAD_DIM)
    v_b = v_b.reshape(b, t, FOX_HEADS, HEAD_DIM)
    logf = jax.nn.log_sigmoid((f_b + forget_b).astype(F32))
    return q_a, kv_a, gates, q_b, k_b, v_b, logf


def even_prompt(z, gate_b, pe, phi, forget_b, slopes):
    b, s = z.shape[:2]
    q_a, kv_a, gates, q_b, k_b, v_b, logf = split_even(z, gate_b, forget_b)
    o_a = nsa_prompt(q_a, kv_a, gates, pe, phi, slopes)
    o_b = fox_prompt(q_b, k_b, v_b, logf)
    mix = jnp.concatenate([o_a.reshape(b, s, -1), o_b.reshape(b, s, -1)], axis=-1)
    wb = min(NSA_WINDOW, s)
    return mix, kv_a[:, :, :4], jnp.stack([k_b, v_b], axis=2), logf, kv_a[:, -wb:, 4:6]


def even_sample(z, cache_nsa_l, cache_fox_kv_l, cache_fox_logf_l, win_buf, page_table, gate_b, pe, phi, forget_b, slopes):
    b, t = z.shape[:2]
    q_a, kv_a, gates, q_b, k_b, v_b, logf = split_even(z, gate_b, forget_b)
    o_a, new_win = nsa_sample(q_a, kv_a, gates, cache_nsa_l, win_buf, page_table, pe, phi, slopes)
    o_b = fox_sample(q_b, k_b, v_b, logf, cache_fox_kv_l, cache_fox_logf_l, page_table)
    mix = jnp.concatenate([o_a.reshape(b, t, -1), o_b.reshape(b, t, -1)], axis=-1)
    return mix, kv_a[:, :, :4], jnp.stack([k_b, v_b], axis=2), logf, new_win


def split_odd(z):
    b, t = z.shape[:2]
    q, k, v = split_cols(z, ODD_SPLITS)
    return (q.reshape(b, t, SWA_KV, SWA_REP, SWA_HEAD_DIM),
            k.reshape(b, t, SWA_KV, SWA_HEAD_DIM), v.reshape(b, t, SWA_KV, SWA_HEAD_DIM))


def odd_prompt(z, sink, slopes):
    b, s = z.shape[:2]
    q, k, v = split_odd(z)
    o = banded_prompt(q, k, v, SWA_WINDOW, slopes, sink)
    wb = min(SWA_WINDOW, s)
    return o.reshape(b, s, -1), jnp.stack([k[:, -wb:], v[:, -wb:]], axis=2)


def odd_sample(z, win_buf, past, sink, slopes):
    b, t = z.shape[:2]
    q, k, v = split_odd(z)
    w = win_buf.shape[1]
    all_win = jnp.concatenate([win_buf, jnp.stack([k, v], axis=2)], axis=1)
    q_pos = past + jnp.arange(t)
    k_pos = past - w + jnp.arange(w + t)
    o = local_attend(q, all_win[:, :, 0], all_win[:, :, 1], q_pos, k_pos, SWA_WINDOW, slopes, sink)
    return o.reshape(b, t, -1), all_win[:, -w:]


def ffn_and_ple(h, p_i, g, w_up, w_down, w_pp, w_pg):
    gate, up = jnp.split(rms_norm(h, g[2]) @ w_up, 2, axis=-1)
    h = h + rms_norm((jax.nn.silu(gate) * up) @ w_down, g[3])
    return h + rms_norm(p_i @ w_pp, g[4]) * jax.nn.sigmoid(h @ w_pg)


def setup_inputs(seed: int = 0) -> dict:
    key = jax.random.key(seed)
    ks = iter(jax.random.split(key, 32))
    n_pages = PAST_LEN // PAGE_SIZE
    n_pool = (5 * DEC_BATCH * n_pages + 3) // 4

    def nrm(shape, scale):
        return jax.random.normal(next(ks), shape, F32) * scale

    x_prompt = nrm((BATCH, SEQ, D_MODEL), 1.0)
    x_sample = nrm((DEC_BATCH, DEC_SEQ, D_MODEL), 1.0)
    cache_nsa_kv = nrm((N_EVEN, n_pool, PAGE_SIZE, 4, NSA_KV, HEAD_DIM), 1.0)
    cache_fox_kv = nrm((N_EVEN, n_pool, PAGE_SIZE, 2, FOX_HEADS, HEAD_DIM), 1.0)
    cache_fox_logf = jax.nn.log_sigmoid(4.0 + nrm((N_EVEN, n_pool, PAGE_SIZE, FOX_HEADS), 1.0))
    state_nsa_win = nrm((N_EVEN, DEC_BATCH, min(NSA_WINDOW, PAST_LEN), 2, NSA_KV, HEAD_DIM), 1.0)
    state_swa_win = nrm((N_ODD, DEC_BATCH, min(SWA_WINDOW, PAST_LEN), 2, SWA_KV, SWA_HEAD_DIM), 1.0)
    page_table = jax.random.permutation(next(ks), n_pool)[: DEC_BATCH * n_pages].reshape(DEC_BATCH, n_pages).astype(jnp.int32)
    p_prompt = nrm((DEPTH, BATCH, SEQ, PLE_DIM), 1.0)
    p_sample = nrm((DEPTH, DEC_BATCH, DEC_SEQ, PLE_DIM), 1.0)
    w_in_even = nrm((N_EVEN, D_MODEL, EVEN_IN), D_MODEL ** -0.5)
    w_out_even = nrm((N_EVEN, EVEN_OUT, D_MODEL), EVEN_OUT ** -0.5)
    nsa_gate_b = nrm((N_EVEN, 3 * NSA_HEADS), 0.1)
    nsa_pe = nrm((N_EVEN, 2, NSA_BLOCK, HEAD_DIM), 0.1)
    nsa_phi = nrm((N_EVEN, 2, HEAD_DIM, HEAD_DIM), HEAD_DIM ** -0.5)
    fox_forget_b = jnp.linspace(2.0, 6.0, FOX_HEADS, dtype=F32)[None, :] + nrm((N_EVEN, FOX_HEADS), 0.1)
    w_in_odd = nrm((N_ODD, D_MODEL, ODD_IN), D_MODEL ** -0.5)
    w_out_odd = nrm((N_ODD, ODD_OUT, D_MODEL), ODD_OUT ** -0.5)
    swa_sink = nrm((N_ODD, SWA_HEADS), 1.0)
    norm_g = 1.0 + nrm((DEPTH, N_NORMS, D_MODEL), 0.05)
    w_ffn_up = nrm((DEPTH, D_MODEL, 2 * D_FF), D_MODEL ** -0.5)
    w_ffn_down = nrm((DEPTH, D_FF, D_MODEL), D_FF ** -0.5)
    w_ple_proj = nrm((DEPTH, PLE_DIM, D_MODEL), PLE_DIM ** -0.5)
    w_ple_gate = nrm((DEPTH, D_MODEL, D_MODEL), D_MODEL ** -0.5)
    return {'x_prompt': x_prompt, 'x_sample': x_sample, 'cache_nsa_kv': cache_nsa_kv,
            'cache_fox_kv': cache_fox_kv, 'cache_fox_logf': cache_fox_logf,
            'state_nsa_win': state_nsa_win, 'state_swa_win': state_swa_win, 'page_table': page_table,
            'p_prompt': p_prompt, 'p_sample': p_sample, 'w_in_even': w_in_even, 'w_out_even': w_out_even,
            'nsa_gate_b': nsa_gate_b, 'nsa_pe': nsa_pe, 'nsa_phi': nsa_phi, 'fox_forget_b': fox_forget_b,
            'w_in_odd': w_in_odd, 'w_out_odd': w_out_odd, 'swa_sink': swa_sink, 'norm_g': norm_g,
            'w_ffn_up': w_ffn_up, 'w_ffn_down': w_ffn_down, 'w_ple_proj': w_ple_proj, 'w_ple_gate': w_ple_gate}


def reference(x_prompt, x_sample, cache_nsa_kv, cache_fox_kv, cache_fox_logf, state_nsa_win, state_swa_win,
              page_table, p_prompt, p_sample, w_in_even, w_out_even, nsa_gate_b, nsa_pe, nsa_phi, fox_forget_b,
              w_in_odd, w_out_odd, swa_sink, norm_g, w_ffn_up, w_ffn_down, w_ple_proj, w_ple_gate):
    nsa_slopes = alibi_slopes(NSA_HEADS).reshape(NSA_KV, NSA_REP)
    swa_slopes = alibi_slopes(SWA_HEADS).reshape(SWA_KV, SWA_REP)
    past = page_table.shape[1] * PAGE_SIZE
    hp, hs = x_prompt, x_sample
    nkv_p, nkv_s, fkv_p, fkv_s, flf_p, flf_s, nw_p, nw_s, sw_p, sw_s = ([] for _ in range(10))
    for i in range(DEPTH):
        g = norm_g[i]
        ap, as_ = rms_norm(hp, g[0]), rms_norm(hs, g[0])
        if i % 2 == 0:
            e = i // 2
            mp, a0, a1, a2, a3 = even_prompt(ap @ w_in_even[e], nsa_gate_b[e], nsa_pe[e], nsa_phi[e],
                                             fox_forget_b[e], nsa_slopes)
            ms, b0, b1, b2, b3 = even_sample(as_ @ w_in_even[e], cache_nsa_kv[e], cache_fox_kv[e], cache_fox_logf[e],
                                             state_nsa_win[e], page_table, nsa_gate_b[e], nsa_pe[e], nsa_phi[e],
                                             fox_forget_b[e], nsa_slopes)
            nkv_p.append(a0); fkv_p.append(a1); flf_p.append(a2); nw_p.append(a3)
            nkv_s.append(b0); fkv_s.append(b1); flf_s.append(b2); nw_s.append(b3)
            w_out = w_out_even[e]
        else:
            o = i // 2
            sink = swa_sink[o].reshape(SWA_KV, SWA_REP)
            mp, a0 = odd_prompt(ap @ w_in_odd[o], sink, swa_slopes)
            ms, b0 = odd_sample(as_ @ w_in_odd[o], state_swa_win[o], past, sink, swa_slopes)
            sw_p.append(a0); sw_s.append(b0)
            w_out = w_out_odd[o]
        hp = hp + rms_norm(mp @ w_out, g[1])
        hs = hs + rms_norm(ms @ w_out, g[1])
        hp = ffn_and_ple(hp, p_prompt[i], g, w_ffn_up[i], w_ffn_down[i], w_ple_proj[i], w_ple_gate[i])
        hs = ffn_and_ple(hs, p_sample[i], g, w_ffn_up[i], w_ffn_down[i], w_ple_proj[i], w_ple_gate[i])
    return (hp, hs, jnp.stack(nkv_p), jnp.stack(nkv_s), jnp.stack(fkv_p), jnp.stack(fkv_s),
            jnp.stack(flf_p), jnp.stack(flf_s), jnp.stack(nw_p), jnp.stack(nw_s), jnp.stack(sw_p), jnp.stack(sw_s))
```

```python
import functools

import jax
import jax.numpy as jnp
from jax import lax
from jax.experimental import pallas as pl
from jax.experimental.pallas import tpu as pltpu

F32 = jnp.float32
BF16 = jnp.bfloat16

HEAD_DIM = 128
NSA_HEADS = 8
NSA_KV = 2
NSA_REP = NSA_HEADS // NSA_KV
NSA_BLOCK = 64
NSA_BLOCK_SHIFT = 6
NSA_N_SEL = 16
NSA_WINDOW = 512
NSA_FORCE = 1e4
FOX_HEADS = 8
SWA_HEADS = 32
SWA_KV = 4
SWA_REP = SWA_HEADS // SWA_KV
SWA_HEAD_DIM = 64
SWA_WINDOW = 128
PAGE_SIZE = 128
EPS = 1e-6

LANES = 128
MASKED = -1e30
VMEM_LIMIT = 56 * 1024 * 1024

E_QA = 0
E_KVA = NSA_HEADS * HEAD_DIM
E_QB = E_KVA + 6 * NSA_KV * HEAD_DIM
E_KB = E_QB + FOX_HEADS * HEAD_DIM
E_VB = E_KB + FOX_HEADS * HEAD_DIM
E_GATE = E_VB + FOX_HEADS * HEAD_DIM
E_FORGET_LANE = 3 * NSA_HEADS
E_COLS = 6144
O_Q = 0
O_K = SWA_HEADS * SWA_HEAD_DIM
O_V = O_K + SWA_KV * LANES
O_COLS = O_V + SWA_KV * LANES


def _cparams(sem):
    return pltpu.CompilerParams(dimension_semantics=sem, vmem_limit_bytes=VMEM_LIMIT)


def _pick(n, pref):
    t = min(n, pref)
    while n % t:
        t //= 2
    return t


def _rms(x, g):
    return x * lax.rsqrt(jnp.mean(x * x, axis=-1, keepdims=True) + EPS) * g


def _split3(x):
    hi = x.astype(BF16)
    r = x - hi.astype(F32)
    mid = r.astype(BF16)
    lo = (r - mid.astype(F32)).astype(BF16)
    return hi, mid, lo


def _prenorm_mm_kernel(x_ref, g_ref, w_ref, o_ref, xn_ref):
    @pl.when(pl.program_id(1) == 0)
    def _():
        xn_ref[...] = _rms(x_ref[...], g_ref[...]).astype(BF16)

    o_ref[...] = jnp.dot(xn_ref[...], w_ref[...], preferred_element_type=F32)


def prenorm_matmul(x, g, w, *, tm=512, tn=1024):
    m, k = x.shape
    n = w.shape[1]
    tm, tn = _pick(m, tm), _pick(n, tn)
    return pl.pallas_call(
        _prenorm_mm_kernel,
        out_shape=jax.ShapeDtypeStruct((m, n), F32),
        grid=(m // tm, n // tn),
        in_specs=[pl.BlockSpec((tm, k), lambda i, j: (i, 0)),
                  pl.BlockSpec((1, k), lambda i, j: (0, 0)),
                  pl.BlockSpec((k, tn), lambda i, j: (0, j))],
        out_specs=pl.BlockSpec((tm, tn), lambda i, j: (i, j)),
        scratch_shapes=[pltpu.VMEM((tm, k), BF16)],
        compiler_params=_cparams(("parallel", "arbitrary")),
        name="prenorm_matmul",
    )(x, g.reshape(1, k), w)


def _prenorm_swiglu_kernel(x_ref, g_ref, wg_ref, wu_ref, o_ref, xn_ref):
    @pl.when(pl.program_id(1) == 0)
    def _():
        xn_ref[...] = _rms(x_ref[...], g_ref[...]).astype(BF16)

    xn = xn_ref[...]
    gate = jnp.dot(xn, wg_ref[...], preferred_element_type=F32)
    up = jnp.dot(xn, wu_ref[...], preferred_element_type=F32)
    o_ref[...] = (gate * jax.nn.sigmoid(gate) * up).astype(BF16)


def prenorm_swiglu(x, g, w_up, *, tm=512, tn=512):
    m, k = x.shape
    f = w_up.shape[1] // 2
    tm, tn = _pick(m, tm), _pick(f, tn)
    nj = f // tn
    return pl.pallas_call(
        _prenorm_swiglu_kernel,
        out_shape=jax.ShapeDtypeStruct((m, f), BF16),
        grid=(m // tm, nj),
        in_specs=[pl.BlockSpec((tm, k), lambda i, j: (i, 0)),
                  pl.BlockSpec((1, k), lambda i, j: (0, 0)),
                  pl.BlockSpec((k, tn), lambda i, j: (0, j)),
                  pl.BlockSpec((k, tn), lambda i, j: (0, j + nj))],
        out_specs=pl.BlockSpec((tm, tn), lambda i, j: (i, j)),
        scratch_shapes=[pltpu.VMEM((tm, k), BF16)],
        compiler_params=_cparams(("parallel", "arbitrary")),
        name="prenorm_swiglu",
    )(x, g.reshape(1, k), w_up, w_up)


def _mm_postnorm_res_kernel(a_ref, w_ref, g_ref, h_ref, o_ref, acc_ref):
    kk = pl.program_id(1)

    @pl.when(kk == 0)
    def _():
        acc_ref[...] = jnp.zeros_like(acc_ref)

    acc_ref[...] += jnp.dot(a_ref[...].astype(BF16), w_ref[...], preferred_element_type=F32)

    @pl.when(kk == pl.num_programs(1) - 1)
    def _():
        o_ref[...] = h_ref[...] + _rms(acc_ref[...], g_ref[...])


def matmul_postnorm_residual(a, w, g, h, *, tm=256, tk=512):
    m, k = a.shape
    n = w.shape[1]
    tm, tk = _pick(m, tm), _pick(k, tk)
    return pl.pallas_call(
        _mm_postnorm_res_kernel,
        out_shape=jax.ShapeDtypeStruct((m, n), F32),
        grid=(m // tm, k // tk),
        in_specs=[pl.BlockSpec((tm, tk), lambda i, kk: (i, kk)),
                  pl.BlockSpec((tk, n), lambda i, kk: (kk, 0)),
                  pl.BlockSpec((1, n), lambda i, kk: (0, 0)),
                  pl.BlockSpec((tm, n), lambda i, kk: (i, 0))],
        out_specs=pl.BlockSpec((tm, n), lambda i, kk: (i, 0)),
        scratch_shapes=[pltpu.VMEM((tm, n), F32)],
        compiler_params=_cparams(("parallel", "arbitrary")),
        name="matmul_postnorm_residual",
    )(a, w, g.reshape(1, n), h)


def _ple_kernel(h_ref, p_ref, wpp_ref, wpg_ref, g_ref, o_ref):
    h = h_ref[...]
    proj = jnp.dot(p_ref[...].astype(BF16), wpp_ref[...], preferred_element_type=F32)
    gate = jnp.dot(h.astype(BF16), wpg_ref[...], preferred_element_type=F32)
    o_ref[...] = h + _rms(proj, g_ref[...]) * jax.nn.sigmoid(gate)


def ple_update(h, p, w_pp, w_pg, g, *, tm=256):
    m, d = h.shape
    pd = p.shape[1]
    tm = _pick(m, tm)
    return pl.pallas_call(
        _ple_kernel,
        out_shape=jax.ShapeDtypeStruct((m, d), F32),
        grid=(m // tm,),
        in_specs=[pl.BlockSpec((tm, d), lambda i: (i, 0)),
                  pl.BlockSpec((tm, pd), lambda i: (i, 0)),
                  pl.BlockSpec((pd, d), lambda i: (0, 0)),
                  pl.BlockSpec((d, d), lambda i: (0, 0)),
                  pl.BlockSpec((1, d), lambda i: (0, 0))],
        out_specs=pl.BlockSpec((tm, d), lambda i: (i, 0)),
        compiler_params=_cparams(("parallel",)),
        name="ple_update",
    )(h, p, w_pp, w_pg, g.reshape(1, d))


def _mmn_postnorm_res_kernel(*refs, k_sizes):
    n_in = len(k_sizes)
    a_refs, (w_ref, g_ref, h_ref, o_ref) = refs[:n_in], refs[n_in:]
    acc = None
    off = 0
    for a_ref, ks in zip(a_refs, k_sizes):
        part = jnp.dot(a_ref[...].astype(BF16), w_ref[off:off + ks, :], preferred_element_type=F32)
        acc = part if acc is None else acc + part
        off += ks
    o_ref[...] = h_ref[...] + _rms(acc, g_ref[...])


def concat_matmul_postnorm_residual(a_list, w, g, h, *, tm=256):
    m, n = h.shape
    k_sizes = tuple(a.shape[1] for a in a_list)
    tm = _pick(m, tm)
    return pl.pallas_call(
        functools.partial(_mmn_postnorm_res_kernel, k_sizes=k_sizes),
        out_shape=jax.ShapeDtypeStruct((m, n), F32),
        grid=(m // tm,),
        in_specs=[pl.BlockSpec((tm, ks), lambda i: (i, 0)) for ks in k_sizes]
        + [pl.BlockSpec(w.shape, lambda i: (0, 0)),
           pl.BlockSpec((1, n), lambda i: (0, 0)),
           pl.BlockSpec((tm, n), lambda i: (i, 0))],
        out_specs=pl.BlockSpec((tm, n), lambda i: (i, 0)),
        compiler_params=_cparams(("parallel",)),
        name="concat_matmul_postnorm_residual",
    )(*a_list, w, g.reshape(1, n), h)


def _lane_pick(x, lane):
    idx = lax.broadcasted_iota(jnp.int32, x.shape, x.ndim - 1)
    return jnp.sum(jnp.where(idx == lane, x, 0.0), axis=-1, keepdims=True)


def _nsa_cmp_kernel(slopes_ref, q_ref, ksrc_ref, vsrc_ref, pe_ref, phi_ref, gz_ref, gb_ref,
                    o_ref, sel_ref, kc_ref, vc_ref, *, tq, nb, nbp):
    g = pl.program_id(1)
    qi = pl.program_id(2)

    @pl.when(qi == 0)
    def _():
        for src, dst, c in ((ksrc_ref, kc_ref, 0), (vsrc_ref, vc_ref, 1)):
            x = src[...].reshape(nb, NSA_BLOCK, HEAD_DIM) + pe_ref[c][None]
            pooled = jnp.mean(x, axis=1)
            dst[...] = jnp.zeros_like(dst)
            dst[0:nb, :] = jnp.dot(pooled.astype(BF16), phi_ref[c].astype(BF16),
                                   preferred_element_type=F32)

    q = q_ref[...]
    q_pos = qi * tq + lax.broadcasted_iota(jnp.int32, (tq, nbp), 0)
    n_idx = lax.broadcasted_iota(jnp.int32, (tq, nbp), 1)
    complete = (n_idx + 1) * NSA_BLOCK - 1 <= q_pos
    valid = complete & (n_idx < nb)
    dist = q_pos.astype(F32) - (n_idx.astype(F32) * NSA_BLOCK + (NSA_BLOCK - 1) / 2)
    kcb = kc_ref[...].astype(BF16)
    vcb = vc_ref[...].astype(BF16)
    gates = jax.nn.sigmoid(gz_ref[...] + gb_ref[...])
    scale = HEAD_DIM ** -0.5
    imp = jnp.zeros((tq, nbp), F32)
    for r in range(NSA_REP):
        qr = q[:, r * HEAD_DIM:(r + 1) * HEAD_DIM].astype(BF16)
        s = lax.dot_general(qr, kcb, (((1,), (1,)), ((), ())), preferred_element_type=F32) * scale
        s = s - slopes_ref[g * NSA_REP + r] * dist
        s = jnp.where(valid, s, MASKED)
        m = jnp.max(s, axis=-1, keepdims=True)
        m = jnp.where(m > 0.5 * MASKED, m, 0.0)
        e = jnp.exp(s - m)
        p = e / jnp.maximum(jnp.sum(e, axis=-1, keepdims=True), 1e-30)
        imp = imp + p
        o_r = jnp.dot(p.astype(BF16), vcb, preferred_element_type=F32)
        o_ref[:, r * HEAD_DIM:(r + 1) * HEAD_DIM] = _lane_pick(gates, g * NSA_REP + r) * o_r

    cur = jnp.right_shift(q_pos, NSA_BLOCK_SHIFT)
    forced = (n_idx == 0) | (n_idx == cur) | (n_idx == cur - 1)
    score = jnp.where(forced, NSA_FORCE, jnp.where(complete, imp, -NSA_FORCE))
    score = jnp.where(n_idx < nb, score, -jnp.inf)
    rank = jnp.zeros((tq, nbp), F32)
    for i in range(nb):
        ci = score[:, i:i + 1]
        rank = rank + jnp.where((ci > score) | ((ci == score) & (n_idx > i)), 1.0, 0.0)
    sel_ref[...] = jnp.where((rank < min(NSA_N_SEL, nb)) & (n_idx < nb), 1.0, 0.0)


def nsa_compress_select(z, slopes, pe, phi, gate_b, *, tq=256):
    b, s, _ = z.shape
    nb = s // NSA_BLOCK
    assert s % NSA_BLOCK == 0 and nb % 8 == 0 and nb <= LANES
    tq = _pick(s, tq)
    cb = lambda off: off // HEAD_DIM
    kern = functools.partial(_nsa_cmp_kernel, tq=tq, nb=nb, nbp=LANES)
    return pl.pallas_call(
        kern,
        out_shape=(jax.ShapeDtypeStruct((b, s, NSA_HEADS * HEAD_DIM), F32),
                   jax.ShapeDtypeStruct((b, NSA_KV, s, LANES), F32)),
        grid=(b, NSA_KV, s // tq),
        in_specs=[pl.BlockSpec(memory_space=pltpu.SMEM),
                  pl.BlockSpec((None, tq, NSA_REP * HEAD_DIM), lambda bi, g, i: (bi, i, g)),
                  pl.BlockSpec((None, s, HEAD_DIM), lambda bi, g, i: (bi, 0, cb(E_KVA) + g)),
                  pl.BlockSpec((None, s, HEAD_DIM), lambda bi, g, i: (bi, 0, cb(E_KVA) + NSA_KV + g)),
                  pl.BlockSpec(pe.shape, lambda bi, g, i: (0, 0, 0)),
                  pl.BlockSpec(phi.shape, lambda bi, g, i: (0, 0, 0)),
                  pl.BlockSpec((None, tq, LANES), lambda bi, g, i: (bi, i, cb(E_GATE))),
                  pl.BlockSpec((1, LANES), lambda bi, g, i: (0, 0))],
        out_specs=(pl.BlockSpec((None, tq, NSA_REP * HEAD_DIM), lambda bi, g, i: (bi, i, g)),
                   pl.BlockSpec((None, None, tq, LANES), lambda bi, g, i: (bi, g, i, 0))),
        scratch_shapes=[pltpu.VMEM((LANES, HEAD_DIM), F32), pltpu.VMEM((LANES, HEAD_DIM), F32)],
        compiler_params=_cparams(("parallel", "parallel", "arbitrary")),
        name="nsa_compress_select",
    )(slopes, z, z, z, pe, phi, z, gate_b)


def _kv_tile(qi, j, *, tq, tk, window):
    q0 = qi * tq
    first = 0 if window is None else lax.div(jnp.maximum(q0 - window, 0), tk)
    last = lax.div(q0 + tq - 1, tk)
    return first + j, last


def _flash_kernel(*refs, mode, tq, tk, rep, nsteps, window, branch):
    it = iter(refs)
    slopes_ref = next(it) if mode in ("sel", "win", "swa") else None
    sink_ref = next(it) if mode == "swa" else None
    q_ref, k_ref, v_ref = next(it), next(it), next(it)
    sel_ref = next(it) if mode == "sel" else None
    fq_ref, fk_ref = (next(it), next(it)) if mode == "fox" else (None, None)
    prev_ref, gz_ref, gb_ref = (next(it), next(it), next(it)) if mode in ("sel", "win") else (None,) * 3
    o_ref, m_ref, l_ref, acc_ref = next(it), next(it), next(it), next(it)

    g = pl.program_id(1)
    qi = pl.program_id(2)
    j = pl.program_id(3)
    kj, kv_last = _kv_tile(qi, j, tq=tq, tk=tk, window=window)
    head_dim = SWA_HEAD_DIM if mode == "swa" else HEAD_DIM
    scale = head_dim ** -0.5

    @pl.when(j == 0)
    def _():
        for r in range(rep):
            if mode == "swa":
                m_ref[r] = jnp.full((tq, 1), sink_ref[g * rep + r], F32)
                l_ref[r] = jnp.ones((tq, 1), F32)
            else:
                m_ref[r] = jnp.full((tq, 1), MASKED, F32)
                l_ref[r] = jnp.zeros((tq, 1), F32)
        acc_ref[...] = jnp.zeros_like(acc_ref)

    @pl.when(kj <= kv_last)
    def _():
        k = k_ref[...].astype(BF16)
        v = v_ref[...].astype(BF16)
        dist_i = (qi * tq + lax.broadcasted_iota(jnp.int32, (tq, tk), 0)
                  - kj * tk - lax.broadcasted_iota(jnp.int32, (tq, tk), 1))
        mask = dist_i >= 0
        if window is not None:
            mask = mask & (dist_i <= window)
        if mode == "sel":
            col_blk = jnp.right_shift(kj * tk + lax.broadcasted_iota(jnp.int32, (LANES, tk), 1),
                                      NSA_BLOCK_SHIFT)
            expand = jnp.where(col_blk == lax.broadcasted_iota(jnp.int32, (LANES, tk), 0), 1.0, 0.0)
            picked = jnp.dot(sel_ref[...].astype(BF16), expand.astype(BF16), preferred_element_type=F32)
            mask = mask & (picked > 0.5)
        if mode == "fox":
            bias = _lane_pick(fq_ref[...], E_FORGET_LANE + g) - fk_ref[...]
        else:
            dist_f = dist_i.astype(F32)
        q = q_ref[...]
        for r in range(rep):
            if mode == "swa":
                qp = q[:, (r // 2) * LANES:(r // 2 + 1) * LANES]
                upper = lax.broadcasted_iota(jnp.int32, qp.shape, 1) >= SWA_HEAD_DIM
                qr = jnp.where(upper if r % 2 else ~upper, qp, 0.0)
            else:
                qr = q[:, r * HEAD_DIM:(r + 1) * HEAD_DIM]
            s = lax.dot_general(qr.astype(BF16), k, (((1,), (1,)), ((), ())),
                                preferred_element_type=F32) * scale
            if mode == "fox":
                s = s + bias
            else:
                s = s - slopes_ref[g * rep + r] * dist_f
            s = jnp.where(mask, s, MASKED)
            m_prev = m_ref[r]
            m_new = jnp.maximum(m_prev, jnp.max(s, axis=-1, keepdims=True))
            alpha = jnp.exp(m_prev - m_new)
            p = jnp.exp(s - m_new)
            l_ref[r] = alpha * l_ref[r] + jnp.sum(p, axis=-1, keepdims=True)
            acc_ref[r] = alpha * acc_ref[r] + jnp.dot(p.astype(BF16), v, preferred_element_type=F32)
            m_ref[r] = m_new

    @pl.when(j == nsteps - 1)
    def _():
        if mode in ("sel", "win"):
            gates = jax.nn.sigmoid(gz_ref[...] + gb_ref[...])
        for r in range(rep):
            o = acc_ref[r] / l_ref[r]
            blk = slice(r * HEAD_DIM, (r + 1) * HEAD_DIM)
            if mode in ("sel", "win"):
                gate = _lane_pick(gates, branch * NSA_HEADS + g * rep + r)
                o_ref[:, blk] = prev_ref[:, blk] + gate * o
            elif mode == "fox":
                o_ref[...] = o
            elif r % 2:
                o_even = acc_ref[r - 1] / l_ref[r - 1]
                upper = lax.broadcasted_iota(jnp.int32, o.shape, 1) >= SWA_HEAD_DIM
                o_ref[:, (r // 2) * LANES:(r // 2 + 1) * LANES] = jnp.where(upper, o, o_even)


def _flash_call(mode, z, *, tq, tk, rep, n_kv, window, branch, q_col, k_col, v_col, out_cols,
                smem=(), extra=(), extra_specs=()):
    b, s, _ = z.shape
    tq, tk = _pick(s, tq), _pick(s, tk)
    nq = s // tq

    def first_last(qi):
        first = 0 if window is None else max(qi * tq - window, 0) // tk
        return first, (qi * tq + tq - 1) // tk

    nsteps = max(l - f + 1 for f, l in map(first_last, range(nq)))
    qw = rep * HEAD_DIM if mode != "swa" else rep * SWA_HEAD_DIM

    def kv_map(col):
        def index(bi, g, i, j):
            kj, last = _kv_tile(i, j, tq=tq, tk=tk, window=window)
            return bi, jnp.minimum(kj, last), col + g
        return index

    in_specs = [pl.BlockSpec(memory_space=pltpu.SMEM) for _ in smem]
    in_specs += [pl.BlockSpec((None, tq, qw), lambda bi, g, i, j: (bi, i, q_col + g)),
                 pl.BlockSpec((None, tk, LANES), kv_map(k_col)),
                 pl.BlockSpec((None, tk, LANES), kv_map(v_col))]
    in_specs += list(extra_specs(tq, tk, kv_map))
    kern = functools.partial(_flash_kernel, mode=mode, tq=tq, tk=tk, rep=rep, nsteps=nsteps,
                             window=window, branch=branch)
    return pl.pallas_call(
        kern,
        out_shape=jax.ShapeDtypeStruct((b, s, out_cols), F32),
        grid=(b, n_kv, nq, nsteps),
        in_specs=in_specs,
        out_specs=pl.BlockSpec((None, tq, qw), lambda bi, g, i, j: (bi, i, g)),
        scratch_shapes=[pltpu.VMEM((rep, tq, 1), F32), pltpu.VMEM((rep, tq, 1), F32),
                        pltpu.VMEM((rep, tq, LANES), F32)],
        compiler_params=_cparams(("parallel", "parallel", "parallel", "arbitrary")),
        name="flash_" + mode,
    )(*smem, z, z, z, *extra)


def _nsa_gate_specs(tq):
    return [pl.BlockSpec((None, tq, NSA_REP * HEAD_DIM), lambda bi, g, i, j: (bi, i, g)),
            pl.BlockSpec((None, tq, LANES), lambda bi, g, i, j: (bi, i, E_GATE // LANES)),
            pl.BlockSpec((1, LANES), lambda bi, g, i, j: (0, 0))]


def nsa_selected(z, sel, prev, slopes, gate_b, *, t=512):
    def specs(tq, tk, kv_map):
        return [pl.BlockSpec((None, None, tq, LANES), lambda bi, g, i, j: (bi, g, i, 0))] + _nsa_gate_specs(tq)
    kvc = E_KVA // HEAD_DIM
    return _flash_call("sel", z, tq=t, tk=t, rep=NSA_REP, n_kv=NSA_KV, window=None, branch=1,
                       q_col=0, k_col=kvc + 2 * NSA_KV, v_col=kvc + 3 * NSA_KV,
                       out_cols=NSA_HEADS * HEAD_DIM, smem=(slopes,), extra=(sel, prev, z, gate_b),
                       extra_specs=specs)


def nsa_window(z, prev, slopes, gate_b, *, t=512):
    def specs(tq, tk, kv_map):
        return _nsa_gate_specs(tq)
    kvc = E_KVA // HEAD_DIM
    return _flash_call("win", z, tq=t, tk=t, rep=NSA_REP, n_kv=NSA_KV, window=NSA_WINDOW, branch=2,
                       q_col=0, k_col=kvc + 4 * NSA_KV, v_col=kvc + 5 * NSA_KV,
                       out_cols=NSA_HEADS * HEAD_DIM, smem=(slopes,), extra=(prev, z, gate_b),
                       extra_specs=specs)


def fox_attention(z, f_cum, f_cum_t, *, t=512):
    def specs(tq, tk, kv_map):
        def fk_map(bi, g, i, j):
            return (bi, g, 0, kv_map(0)(bi, g, i, j)[1])
        return [pl.BlockSpec((None, tq, LANES), lambda bi, g, i, j: (bi, i, 0)),
                pl.BlockSpec((None, None, 1, tk), fk_map)]
    return _flash_call("fox", z, tq=t, tk=t, rep=1, n_kv=FOX_HEADS, window=None, branch=0,
                       q_col=E_QB // HEAD_DIM, k_col=E_KB // HEAD_DIM, v_col=E_VB // HEAD_DIM,
                       out_cols=FOX_HEADS * HEAD_DIM, extra=(f_cum, f_cum_t), extra_specs=specs)


def swa_attention(z, slopes, sink, *, t=256):
    return _flash_call("swa", z, tq=t, tk=t, rep=SWA_REP, n_kv=SWA_KV, window=SWA_WINDOW, branch=0,
                       q_col=0, k_col=O_K // LANES, v_col=O_V // LANES,
                       out_cols=SWA_HEADS * SWA_HEAD_DIM, smem=(slopes, sink),
                       extra_specs=lambda tq, tk, kv_map: [])


def _fox_gates_kernel(gz_ref, fb_ref, logf_ref, f_ref, ft_ref, *, chunk):
    x = gz_ref[...] + fb_ref[...]
    logf = jnp.minimum(x, 0.0) - jnp.log1p(jnp.exp(-jnp.abs(x)))
    logf_ref[...] = logf
    tri = jnp.where(lax.broadcasted_iota(jnp.int32, (chunk, chunk), 0)
                    >= lax.broadcasted_iota(jnp.int32, (chunk, chunk), 1), 1.0, 0.0).astype(BF16)
    carry = jnp.zeros((1, LANES), F32)
    for c in range(logf.shape[0] // chunk):
        rows = slice(c * chunk, (c + 1) * chunk)
        y = carry
        for part in _split3(logf[rows]):
            y = y + jnp.dot(tri, part, preferred_element_type=F32)
        f_ref[rows, :] = y
        carry = y[chunk - 1:chunk, :]
    ft_ref[...] = f_ref[...].T


def fox_gates(z, forget_b_lanes, *, chunk=256):
    b, s, _ = z.shape
    chunk = _pick(s, chunk)
    shp = jax.ShapeDtypeStruct((b, s, LANES), F32)
    return pl.pallas_call(
        functools.partial(_fox_gates_kernel, chunk=chunk),
        out_shape=(shp, shp, jax.ShapeDtypeStruct((b, LANES, s), F32)),
        grid=(b,),
        in_specs=[pl.BlockSpec((None, s, LANES), lambda bi: (bi, 0, E_GATE // LANES)),
                  pl.BlockSpec((1, LANES), lambda bi: (0, 0))],
        out_specs=(pl.BlockSpec((None, s, LANES), lambda bi: (bi, 0, 0)),
                   pl.BlockSpec((None, s, LANES), lambda bi: (bi, 0, 0)),
                   pl.BlockSpec((None, LANES, s), lambda bi: (bi, 0, 0))),
        compiler_params=_cparams(("parallel",)),
        name="fox_gates",
    )(z, forget_b_lanes)


def _alibi_slopes(n):
    return 2.0 ** (-8.0 * jnp.arange(1, n + 1, dtype=F32) / n)


def _prep_even_w_in(w):
    d = w.shape[0]
    n_gate = 3 * NSA_HEADS
    src_gate = E_QB
    src_qb = src_gate + n_gate
    src_forget = src_qb + 3 * FOX_HEADS * HEAD_DIM
    used = E_GATE + n_gate + FOX_HEADS
    return jnp.concatenate([w[:, :src_gate], w[:, src_qb:src_forget], w[:, src_gate:src_qb],
                            w[:, src_forget:], jnp.zeros((d, E_COLS - used), w.dtype)], axis=1).astype(BF16)


def _prep_odd_w_in(w):
    d = w.shape[0]
    kw = SWA_KV * SWA_HEAD_DIM
    k = w[:, O_K:O_K + kw].reshape(d, SWA_KV, SWA_HEAD_DIM)
    v = w[:, O_K + kw:].reshape(d, SWA_KV, SWA_HEAD_DIM)
    dup = lambda x: jnp.concatenate([x, x], axis=-1).reshape(d, SWA_KV * LANES)
    return jnp.concatenate([w[:, :O_K], dup(k), dup(v)], axis=1).astype(BF16)


def _lane_row(values, offset):
    return jnp.zeros((1, LANES), F32).at[0, offset:offset + values.shape[0]].set(values)


def _ffn_and_ple(h, p, g, w_up, w_down, w_pp, w_pg):
    act = prenorm_swiglu(h, g[2], w_up)
    h = matmul_postnorm_residual(act, w_down, g[3], h)
    return ple_update(h, p, w_pp, w_pg, g[4])


def _even_prompt(h, b, s, g, w_in, w_out, gate_b, forget_b, pe, phi, slopes):
    z = prenorm_matmul(h, g[0], w_in).reshape(b, s, E_COLS)
    o_a, sel = nsa_compress_select(z, slopes, pe, phi, gate_b)
    o_a = nsa_selected(z, sel, o_a, slopes, gate_b)
    o_a = nsa_window(z, o_a, slopes, gate_b)
    logf, f_cum, f_cum_t = fox_gates(z, forget_b)
    fl = slice(E_FORGET_LANE, E_FORGET_LANE + FOX_HEADS)
    o_b = fox_attention(z, f_cum, f_cum_t[:, fl, :].reshape(b, FOX_HEADS, 1, s))
    m = b * s
    h = concat_matmul_postnorm_residual([o_a.reshape(m, -1), o_b.reshape(m, -1)], w_out, g[1], h)
    wb = min(NSA_WINDOW, s)
    win_off = E_KVA + 4 * NSA_KV * HEAD_DIM
    outs = (z[:, :, E_KVA:win_off].reshape(b, s, 4, NSA_KV, HEAD_DIM),
            z[:, :, E_KB:E_GATE].reshape(b, s, 2, FOX_HEADS, HEAD_DIM),
            logf[:, :, fl],
            z[:, s - wb:, win_off:E_QB].reshape(b, wb, 2, NSA_KV, HEAD_DIM))
    return h, outs


def _odd_prompt(h, b, s, g, w_in, w_out, sink, slopes):
    z = prenorm_matmul(h, g[0], w_in).reshape(b, s, O_COLS)
    o = swa_attention(z, slopes, sink)
    h = concat_matmul_postnorm_residual([o.reshape(b * s, -1)], w_out, g[1], h)
    wb = min(SWA_WINDOW, s)
    tail = z[:, s - wb:, O_K:].reshape(b, wb, 2, SWA_KV, 2, SWA_HEAD_DIM)[:, :, :, :, 0]
    return h, tail


NSA_POOL_PAGES = 8
FOX_PAGES = 4


def _nsa_pool_kernel(pt_ref, *refs, n_pages):
    page_refs, (pe_ref, o_ref) = refs[:n_pages], refs[n_pages:]
    blocks_per_page = PAGE_SIZE // NSA_BLOCK
    for i, ref in enumerate(page_refs):
        x = ref[...].reshape(blocks_per_page, NSA_BLOCK, ref.shape[-1]) + pe_ref[...][None]
        o_ref[i * blocks_per_page:(i + 1) * blocks_per_page, :] = jnp.mean(x, axis=1)


def nsa_pool_pages(cache, page_table, pe_lanes):
    db, n_pages = page_table.shape
    width = 2 * NSA_KV * HEAD_DIM
    p = _pick(n_pages, NSA_POOL_PAGES)
    bpp = PAGE_SIZE // NSA_BLOCK

    def page_map(i):
        return lambda b, c, pt: (pt[b, c * p + i], 0, 0)

    return pl.pallas_call(
        functools.partial(_nsa_pool_kernel, n_pages=p),
        out_shape=jax.ShapeDtypeStruct((db, n_pages * bpp, width), F32),
        grid_spec=pltpu.PrefetchScalarGridSpec(
            num_scalar_prefetch=1,
            grid=(db, n_pages // p),
            in_specs=[pl.BlockSpec((None, PAGE_SIZE, width), page_map(i)) for i in range(p)]
            + [pl.BlockSpec((NSA_BLOCK, width), lambda b, c, pt: (0, 0))],
            out_specs=pl.BlockSpec((None, p * bpp, width), lambda b, c, pt: (b, c, 0))),
        compiler_params=_cparams(("parallel", "arbitrary")),
        name="nsa_pool_pages",
    )(page_table, *([cache] * p), pe_lanes)


def _transposed_rows(row):
    n = row.shape[1]
    bc = jnp.broadcast_to(row, (n, n))
    return bc.T, bc


def _nsa_sample_cmp_kernel(q_ref, pooled_ref, phi_ref, slopes_ref, gz_ref, gb_ref, o_ref, idx_ref,
                           *, q_pos, n_pick):
    npb = pooled_ref.shape[0]
    q = q_ref[...].astype(BF16)
    pooled = pooled_ref[...].astype(BF16)
    n_row = lax.broadcasted_iota(jnp.int32, (1, npb), 1)
    complete = (n_row + 1) * NSA_BLOCK - 1 <= q_pos
    dist = q_pos - (n_row.astype(F32) * NSA_BLOCK + (NSA_BLOCK - 1) / 2)
    head_row = lax.broadcasted_iota(jnp.int32, (NSA_HEADS, 1), 0)
    gates = jax.nn.sigmoid(gz_ref[...] + gb_ref[...])
    lane = lax.broadcasted_iota(jnp.int32, (NSA_HEADS, LANES), 1)
    gate_cmp = jnp.sum(jnp.where(lane == head_row, gates, 0.0), axis=-1, keepdims=True)
    scale = HEAD_DIM ** -0.5
    i_idx = lax.broadcasted_iota(jnp.int32, (npb, npb), 0)
    j_idx = lax.broadcasted_iota(jnp.int32, (npb, npb), 1)
    o_acc = jnp.zeros((NSA_HEADS, HEAD_DIM), F32)
    for g in range(NSA_KV):
        in_group = (head_row >= g * NSA_REP) & (head_row < (g + 1) * NSA_REP)
        kc = jnp.dot(pooled[:, g * HEAD_DIM:(g + 1) * HEAD_DIM], phi_ref[0].astype(BF16),
                     preferred_element_type=F32).astype(BF16)
        vc = jnp.dot(pooled[:, (NSA_KV + g) * HEAD_DIM:(NSA_KV + g + 1) * HEAD_DIM],
                     phi_ref[1].astype(BF16), preferred_element_type=F32).astype(BF16)
        s = lax.dot_general(q, kc, (((1,), (1,)), ((), ())), preferred_element_type=F32) * scale
        s = s - slopes_ref[...] * dist
        s = jnp.where(complete, s, MASKED)
        m = jnp.max(s, axis=-1, keepdims=True)
        m = jnp.where(m > 0.5 * MASKED, m, 0.0)
        e = jnp.exp(s - m)
        p = e / jnp.maximum(jnp.sum(e, axis=-1, keepdims=True), 1e-30)
        o_g = jnp.dot(p.astype(BF16), vc, preferred_element_type=F32)
        o_acc = jnp.where(in_group, o_g, o_acc)
        imp = jnp.sum(jnp.where(in_group, p, 0.0), axis=0, keepdims=True)

        cur = q_pos // NSA_BLOCK
        forced = (n_row == 0) | (n_row == cur) | (n_row == cur - 1)
        score = jnp.where(forced, NSA_FORCE, jnp.where(complete, imp, -NSA_FORCE))
        col, bc = _transposed_rows(score)
        rank = jnp.sum(jnp.where((col > bc) | ((col == bc) & (i_idx < j_idx)), 1.0, 0.0),
                       axis=0, keepdims=True)
        sel = jnp.where(rank < n_pick, 1.0, 0.0)
        sel_col, _ = _transposed_rows(sel)
        pos = jnp.sum(jnp.where(i_idx < j_idx, sel_col, 0.0), axis=0, keepdims=True)
        slot = lax.broadcasted_iota(jnp.int32, (NSA_N_SEL, npb), 0).astype(F32)
        n_f = lax.broadcasted_iota(jnp.int32, (NSA_N_SEL, npb), 1).astype(F32)
        hit = (sel > 0.5) & (pos == slot)
        ids = jnp.sum(jnp.where(hit, n_f, 0.0), axis=-1, keepdims=True)
        idx_ref[g] = jnp.broadcast_to(ids, (NSA_N_SEL, LANES)).astype(jnp.int32)
    o_ref[...] = gate_cmp * o_acc


def nsa_sample_compress_select(q, pooled, phi, slopes, gz, gate_b, *, q_pos, n_pick):
    db = q.shape[0]
    npb = pooled.shape[1]
    kern = functools.partial(_nsa_sample_cmp_kernel, q_pos=q_pos, n_pick=n_pick)
    return pl.pallas_call(
        kern,
        out_shape=(jax.ShapeDtypeStruct((db, NSA_HEADS, HEAD_DIM), F32),
                   jax.ShapeDtypeStruct((db, NSA_KV, NSA_N_SEL, LANES), jnp.int32)),
        grid=(db,),
        in_specs=[pl.BlockSpec((None, NSA_HEADS, HEAD_DIM), lambda b: (b, 0, 0)),
                  pl.BlockSpec((None, npb, pooled.shape[2]), lambda b: (b, 0, 0)),
                  pl.BlockSpec(phi.shape, lambda b: (0, 0, 0)),
                  pl.BlockSpec((NSA_HEADS, 1), lambda b: (0, 0)),
                  pl.BlockSpec((None, 1, LANES), lambda b: (b, 0, 0)),
                  pl.BlockSpec((1, LANES), lambda b: (0, 0))],
        out_specs=(pl.BlockSpec((None, NSA_HEADS, HEAD_DIM), lambda b: (b, 0, 0)),
                   pl.BlockSpec((None, NSA_KV, NSA_N_SEL, LANES), lambda b: (b, 0, 0, 0))),
        compiler_params=_cparams(("parallel",)),
        name="nsa_sample_compress_select",
    )(q, pooled, phi, slopes.reshape(NSA_HEADS, 1), gz, gate_b)


def _nsa_sample_attend_kernel(idx_ref, pt_ref, *refs, n_pick, q_pos):
    n_blk = 2 * NSA_KV * n_pick
    blk_refs = refs[:n_blk]
    q_ref, new_ref, win_ref, prev_ref, slopes_ref, gz_ref, gb_ref, o_ref, win_out_ref = refs[n_blk:]
    b = pl.program_id(0)
    w = win_ref.shape[0]
    scale = HEAD_DIM ** -0.5
    gates = jax.nn.sigmoid(gz_ref[...] + gb_ref[...])
    r_col = lax.broadcasted_iota(jnp.int32, (NSA_REP, 1), 0)
    lane = lax.broadcasted_iota(jnp.int32, (NSA_REP, LANES), 1)
    row_w = lax.broadcasted_iota(jnp.int32, (w, HEAD_DIM), 0)
    n_keys = n_pick * NSA_BLOCK
    key_lane = lax.broadcasted_iota(jnp.int32, (1, n_keys), 1)

    def rounded(x):
        return x.astype(BF16).astype(F32)

    def attend(q4, keys, vals, k_new, v_new, dist, valid, slope):
        s = lax.dot_general(q4.astype(BF16), keys.astype(BF16), (((1,), (1,)), ((), ())),
                            preferred_element_type=F32) * scale - slope * dist
        s = jnp.where(valid, s, MASKED)
        s_new = jnp.sum(rounded(q4) * rounded(k_new), axis=-1, keepdims=True) * scale
        m = jnp.maximum(jnp.max(s, axis=-1, keepdims=True), s_new)
        e = jnp.exp(s - m)
        e_new = jnp.exp(s_new - m)
        den = jnp.sum(e, axis=-1, keepdims=True) + e_new
        num = jnp.dot(e.astype(BF16), vals.astype(BF16), preferred_element_type=F32)
        return (num + rounded(e_new) * rounded(v_new)) / den

    for g in range(NSA_KV):
        q4 = q_ref[g]
        slope = slopes_ref[g]
        new = lambda c: new_ref[c * NSA_KV + g:c * NSA_KV + g + 1, :]
        gate = lambda c: jnp.sum(jnp.where(lane == c * NSA_HEADS + g * NSA_REP + r_col, gates, 0.0),
                                 axis=-1, keepdims=True)
        kb = blk_refs[(2 * g) * n_pick:(2 * g + 1) * n_pick]
        vb = blk_refs[(2 * g + 1) * n_pick:(2 * g + 2) * n_pick]
        keys = jnp.concatenate([r[...] for r in kb], axis=0)
        vals = jnp.concatenate([r[...] for r in vb], axis=0)
        k_pos = key_lane & (NSA_BLOCK - 1)
        for i in range(n_pick):
            in_blk = jnp.right_shift(key_lane, NSA_BLOCK_SHIFT) == i
            k_pos = k_pos + jnp.where(in_blk, idx_ref[b, g, i] * NSA_BLOCK, 0)
        dist = q_pos - k_pos
        o_sel = attend(q4, keys, vals, new(2), new(3), dist.astype(F32), dist >= 0, slope)
        k_w = win_ref[:, g * HEAD_DIM:(g + 1) * HEAD_DIM]
        v_w = win_ref[:, (NSA_KV + g) * HEAD_DIM:(NSA_KV + g + 1) * HEAD_DIM]
        dist_w = w - lax.broadcasted_iota(jnp.int32, (1, w), 1)
        o_win = attend(q4, k_w, v_w, new(4), new(5), dist_w.astype(F32), dist_w <= NSA_WINDOW, slope)
        o_ref[g] = prev_ref[g] + gate(1) * o_sel + gate(2) * o_win
        for c, buf in ((0, k_w), (1, v_w)):
            rolled = pltpu.roll(buf, shift=w - 1, axis=0)
            col = (c * NSA_KV + g) * HEAD_DIM
            win_out_ref[:, col:col + HEAD_DIM] = jnp.where(row_w == w - 1, new(4 + c), rolled)


def nsa_sample_attend(cache, idx, page_table, q, new_rows, win, prev, slopes, gz, gate_b, *, n_pick, q_pos):
    db = q.shape[0]
    w = win.shape[1]
    bpp_shift = (PAGE_SIZE // NSA_BLOCK).bit_length() - 1

    def blk_map(g, i, comp):
        def index(b, idx_r, pt_r):
            blk = idx_r[b, g, i]
            return (pt_r[b, jnp.right_shift(blk, bpp_shift)], blk & (PAGE_SIZE // NSA_BLOCK - 1),
                    comp * NSA_KV + g)
        return index

    blk_specs = [pl.BlockSpec((None, NSA_BLOCK, HEAD_DIM), blk_map(g, i, comp))
                 for g in range(NSA_KV) for comp in (2, 3) for i in range(n_pick)]
    full = lambda shape: pl.BlockSpec(shape, lambda b, idx_r, pt_r: (0,) * len(shape))
    per_b = lambda shape: pl.BlockSpec((None,) + shape, lambda b, idx_r, pt_r: (b,) + (0,) * len(shape))
    kern = functools.partial(_nsa_sample_attend_kernel, n_pick=n_pick, q_pos=q_pos)
    return pl.pallas_call(
        kern,
        out_shape=(jax.ShapeDtypeStruct((db, NSA_KV, NSA_REP, HEAD_DIM), F32),
                   jax.ShapeDtypeStruct(win.shape, F32)),
        grid_spec=pltpu.PrefetchScalarGridSpec(
            num_scalar_prefetch=2,
            grid=(db,),
            in_specs=blk_specs + [per_b((NSA_KV, NSA_REP, HEAD_DIM)), per_b(new_rows.shape[1:]),
                                  per_b(win.shape[1:]), per_b((NSA_KV, NSA_REP, HEAD_DIM)),
                                  full((NSA_KV, NSA_REP, 1)), per_b((1, LANES)), full((1, LANES))],
            out_specs=(per_b((NSA_KV, NSA_REP, HEAD_DIM)), per_b(win.shape[1:]))),
        compiler_params=_cparams(("arbitrary",)),
        name="nsa_sample_attend",
    )(idx, page_table, *([cache] * len(blk_specs)), q, new_rows, win, prev,
      slopes.reshape(NSA_KV, NSA_REP, 1), gz, gate_b)


def _fox_decode_kernel(pt_ref, *refs, n_pages):
    kv_refs, lf_refs = refs[:n_pages], refs[n_pages:2 * n_pages]
    (q_ref, knew_ref, vnew_ref, vrow_ref, fraw_ref, fb_ref,
     o_ref, lfnew_ref, qbd_ref, m_ref, l_ref, acc_ref, carry_ref) = refs[2 * n_pages:]
    c = pl.program_id(1)
    width = FOX_HEADS * HEAD_DIM
    scale = HEAD_DIM ** -0.5
    head_row = lax.broadcasted_iota(jnp.int32, (FOX_HEADS, width), 0)
    head_of_lane = jnp.right_shift(lax.broadcasted_iota(jnp.int32, (FOX_HEADS, width), 1),
                                   HEAD_DIM.bit_length() - 1)

    @pl.when(c == 0)
    def _():
        q = q_ref[...]
        qbd_ref[...] = jnp.where(head_of_lane == head_row, jnp.concatenate([q] * FOX_HEADS, axis=1),
                                 0.0).astype(BF16)
        x = fraw_ref[...] + fb_ref[...]
        lf_new = jnp.minimum(x, 0.0) - jnp.log1p(jnp.exp(-jnp.abs(x)))
        lfnew_ref[...] = lf_new
        carry_ref[...] = lf_new
        rounded = lambda t: t.astype(BF16).astype(F32)
        m_ref[...] = jnp.sum(rounded(q) * rounded(knew_ref[...]), axis=-1, keepdims=True) * scale
        l_ref[...] = jnp.ones_like(l_ref)
        acc_ref[...] = jnp.broadcast_to(rounded(vrow_ref[...]), acc_ref.shape)

    later = jnp.where(lax.broadcasted_iota(jnp.int32, (PAGE_SIZE, PAGE_SIZE), 0)
                      > lax.broadcasted_iota(jnp.int32, (PAGE_SIZE, PAGE_SIZE), 1), 1.0, 0.0).astype(BF16)
    for kv_ref, lf_ref in zip(kv_refs, lf_refs):
        lf = lf_ref[...]
        bias = carry_ref[...]
        for part in _split3(lf):
            bias = bias + jnp.dot(part, later, preferred_element_type=F32)
        carry_ref[...] = carry_ref[...] + jnp.sum(lf, axis=-1, keepdims=True)
        kp = kv_ref[:, :width].astype(BF16)
        vp = kv_ref[:, width:].astype(BF16)
        s = lax.dot_general(qbd_ref[...], kp, (((1,), (1,)), ((), ())),
                            preferred_element_type=F32) * scale + bias
        m_prev = m_ref[...]
        m_new = jnp.maximum(m_prev, jnp.max(s, axis=-1, keepdims=True))
        alpha = jnp.exp(m_prev - m_new)
        p = jnp.exp(s - m_new)
        l_ref[...] = alpha * l_ref[...] + jnp.sum(p, axis=-1, keepdims=True)
        acc_ref[...] = alpha * acc_ref[...] + jnp.dot(p.astype(BF16), vp, preferred_element_type=F32)
        m_ref[...] = m_new

    @pl.when(c == pl.num_programs(1) - 1)
    def _():
        acc = jnp.where(head_of_lane == head_row, acc_ref[...], 0.0)
        o = acc[:, :HEAD_DIM]
        for h in range(1, FOX_HEADS):
            o = o + acc[:, h * HEAD_DIM:(h + 1) * HEAD_DIM]
        o_ref[...] = o / l_ref[...]


def fox_decode(cache_kv, cache_logf_t, page_table, q, k_new, v_new, f_raw, forget_b):
    db, n_pages = page_table.shape
    p = _pick(n_pages, FOX_PAGES)
    width = FOX_HEADS * HEAD_DIM

    def page_map(i):
        return lambda b, c, pt: (pt[b, n_pages - 1 - (c * p + i)], 0, 0)

    per_b = lambda shape: pl.BlockSpec((None,) + shape, lambda b, c, pt: (b,) + (0,) * len(shape))
    return pl.pallas_call(
        functools.partial(_fox_decode_kernel, n_pages=p),
        out_shape=(jax.ShapeDtypeStruct((db, FOX_HEADS, HEAD_DIM), F32),
                   jax.ShapeDtypeStruct((db, FOX_HEADS, 1), F32)),
        grid_spec=pltpu.PrefetchScalarGridSpec(
            num_scalar_prefetch=1,
            grid=(db, n_pages // p),
            in_specs=[pl.BlockSpec((None, PAGE_SIZE, 2 * width), page_map(i)) for i in range(p)]
            + [pl.BlockSpec((None, FOX_HEADS, PAGE_SIZE), page_map(i)) for i in range(p)]
            + [per_b((FOX_HEADS, HEAD_DIM)), per_b((FOX_HEADS, HEAD_DIM)), per_b((FOX_HEADS, HEAD_DIM)),
               per_b((1, width)), per_b((FOX_HEADS, 1)),
               pl.BlockSpec((FOX_HEADS, 1), lambda b, c, pt: (0, 0))],
            out_specs=(per_b((FOX_HEADS, HEAD_DIM)), per_b((FOX_HEADS, 1))),
            scratch_shapes=[pltpu.VMEM((FOX_HEADS, width), BF16), pltpu.VMEM((FOX_HEADS, 1), F32),
                            pltpu.VMEM((FOX_HEADS, 1), F32), pltpu.VMEM((FOX_HEADS, width), F32),
                            pltpu.VMEM((FOX_HEADS, 1), F32)]),
        compiler_params=_cparams(("parallel", "arbitrary")),
        name="fox_decode",
    )(page_table, *([cache_kv] * p), *([cache_logf_t] * p), q, k_new, v_new,
      v_new.reshape(db, 1, width), f_raw, forget_b.reshape(FOX_HEADS, 1))


def _swa_decode_kernel(q_ref, new_ref, win_ref, slopes_ref, sink_ref, o_ref, win_out_ref):
    w = win_ref.shape[0]
    kvw = SWA_KV * SWA_HEAD_DIM
    scale = SWA_HEAD_DIM ** -0.5
    rounded = lambda t: t.astype(BF16).astype(F32)
    dist = (w - lax.broadcasted_iota(jnp.int32, (1, w), 1))
    new = new_ref[...]
    for g in range(SWA_KV):
        q = q_ref[g]
        lo = g * SWA_HEAD_DIM
        k_w = win_ref[:, lo:lo + SWA_HEAD_DIM]
        v_w = win_ref[:, kvw + lo:kvw + lo + SWA_HEAD_DIM]
        k_new = new[:, lo:lo + SWA_HEAD_DIM]
        v_new = new[:, kvw + lo:kvw + lo + SWA_HEAD_DIM]
        s = lax.dot_general(q.astype(BF16), k_w.astype(BF16), (((1,), (1,)), ((), ())),
                            preferred_element_type=F32) * scale - slopes_ref[g] * dist.astype(F32)
        s = jnp.where(dist <= SWA_WINDOW, s, MASKED)
        s_new = jnp.sum(rounded(q) * rounded(k_new), axis=-1, keepdims=True) * scale
        sink = sink_ref[g]
        m = jnp.maximum(jnp.maximum(jnp.max(s, axis=-1, keepdims=True), s_new), sink)
        e = jnp.exp(s - m)
        e_new = jnp.exp(s_new - m)
        den = jnp.sum(e, axis=-1, keepdims=True) + e_new + jnp.exp(sink - m)
        num = jnp.dot(e.astype(BF16), v_w.astype(BF16), preferred_element_type=F32)
        o_ref[g] = (num + rounded(e_new) * rounded(v_new)) / den
    buf = win_ref[...]
    row = lax.broadcasted_iota(jnp.int32, buf.shape, 0)
    win_out_ref[...] = jnp.where(row == w - 1, new, pltpu.roll(buf, shift=w - 1, axis=0))


def swa_decode(q, new_row, win, slopes, sink):
    db = q.shape[0]
    per_b = lambda shape: pl.BlockSpec((None,) + shape, lambda b: (b,) + (0,) * len(shape))
    full = lambda shape: pl.BlockSpec(shape, lambda b: (0,) * len(shape))
    return pl.pallas_call(
        _swa_decode_kernel,
        out_shape=(jax.ShapeDtypeStruct(q.shape, F32), jax.ShapeDtypeStruct(win.shape, F32)),
        grid=(db,),
        in_specs=[per_b(q.shape[1:]), per_b(new_row.shape[1:]), per_b(win.shape[1:]),
                  full((SWA_KV, SWA_REP, 1)), full((SWA_KV, SWA_REP, 1))],
        out_specs=(per_b(q.shape[1:]), per_b(win.shape[1:])),
        compiler_params=_cparams(("parallel",)),
        name="swa_decode",
    )(q, new_row, win, slopes.reshape(SWA_KV, SWA_REP, 1), sink.reshape(SWA_KV, SWA_REP, 1))


def _even_sample(h, g, w_in, w_out, gate_b, forget_b, pe, phi, slopes,
                 cache_nsa, cache_fox_kv, cache_fox_logf, win_buf, page_table):
    db = h.shape[0]
    past = page_table.shape[1] * PAGE_SIZE
    n_pick = NSA_N_SEL - 1
    assert past % NSA_BLOCK == 0 and past // NSA_BLOCK >= n_pick
    pool = cache_nsa.shape[0]
    z = prenorm_matmul(h, g[0], w_in)
    heads = lambda lo, hi: z[:, lo:hi].reshape(db, -1, HEAD_DIM)
    gz = z[:, E_GATE:E_GATE + LANES].reshape(db, 1, LANES)
    q_a = heads(E_QA, E_KVA)
    cache_n = cache_nsa.reshape(pool, PAGE_SIZE, 4 * NSA_KV * HEAD_DIM)
    pe_lanes = jnp.concatenate([pe[0]] * NSA_KV + [pe[1]] * NSA_KV, axis=1)
    pooled = nsa_pool_pages(cache_n, page_table, pe_lanes)
    o_cmp, idx = nsa_sample_compress_select(q_a, pooled, phi, slopes, gz, gate_b, q_pos=past, n_pick=n_pick)
    grp = lambda x: x.reshape(db, NSA_KV, NSA_REP, HEAD_DIM)
    w = win_buf.shape[1]
    o_a, new_win = nsa_sample_attend(cache_n, idx[..., 0], page_table, grp(q_a), heads(E_KVA, E_QB),
                                     win_buf.reshape(db, w, -1), grp(o_cmp), slopes, gz, gate_b,
                                     n_pick=n_pick, q_pos=past)
    o_b, lf_new = fox_decode(cache_fox_kv.reshape(pool, PAGE_SIZE, -1), cache_fox_logf.transpose(0, 2, 1),
                             page_table, heads(E_QB, E_KB), heads(E_KB, E_VB), heads(E_VB, E_GATE),
                             z[:, E_GATE + E_FORGET_LANE:E_GATE + E_FORGET_LANE + FOX_HEADS].reshape(db, -1, 1),
                             forget_b)
    h = concat_matmul_postnorm_residual([o_a.reshape(db, -1), o_b.reshape(db, -1)], w_out, g[1], h)
    win_off = E_KVA + 4 * NSA_KV * HEAD_DIM
    outs = (z[:, E_KVA:win_off].reshape(db, 1, 4, NSA_KV, HEAD_DIM),
            z[:, E_KB:E_GATE].reshape(db, 1, 2, FOX_HEADS, HEAD_DIM),
            lf_new.reshape(db, 1, FOX_HEADS),
            new_win.reshape(win_buf.shape))
    return h, outs


def _odd_sample(h, g, w_in, w_out, sink, slopes, win_buf):
    db = h.shape[0]
    w = win_buf.shape[1]
    z = prenorm_matmul(h, g[0], w_in)
    q = z[:, :O_K].reshape(db, SWA_KV, SWA_REP, SWA_HEAD_DIM)
    new_row = z[:, O_K:].reshape(db, 2, SWA_KV, 2, SWA_HEAD_DIM)[:, :, :, 0].reshape(db, 1, -1)
    o, new_win = swa_decode(q, new_row, win_buf.reshape(db, w, -1), slopes, sink)
    h = concat_matmul_postnorm_residual([o.reshape(db, -1)], w_out, g[1], h)
    return h, new_win.reshape(win_buf.shape)


def kernel(x_prompt, x_sample, cache_nsa_kv, cache_fox_kv, cache_fox_logf, state_nsa_win, state_swa_win,
           page_table, p_prompt, p_sample, w_in_even, w_out_even, nsa_gate_b, nsa_pe, nsa_phi, fox_forget_b,
           w_in_odd, w_out_odd, swa_sink, norm_g, w_ffn_up, w_ffn_down, w_ple_proj, w_ple_gate):
    b, s, d = x_prompt.shape
    db, dec_seq, _ = x_sample.shape
    assert dec_seq == 1, "the decode kernels handle one new token per sequence"
    depth = norm_g.shape[0]
    nsa_slopes = _alibi_slopes(NSA_HEADS)
    swa_slopes = _alibi_slopes(SWA_HEADS)
    hp = x_prompt.reshape(b * s, d)
    hs = x_sample.reshape(db, d)
    even_p, even_s, odd_p, odd_s = [], [], [], []
    for i in range(depth):
        g = norm_g[i]
        if i % 2 == 0:
            e = i // 2
            w_in = _prep_even_w_in(w_in_even[e])
            w_out = w_out_even[e].astype(BF16)
            gate_b = _lane_row(nsa_gate_b[e], 0)
            hp, outs = _even_prompt(hp, b, s, g, w_in, w_out, gate_b,
                                    _lane_row(fox_forget_b[e], E_FORGET_LANE), nsa_pe[e], nsa_phi[e], nsa_slopes)
            even_p.append(outs)
            hs, outs = _even_sample(hs, g, w_in, w_out, gate_b, fox_forget_b[e], nsa_pe[e], nsa_phi[e],
                                    nsa_slopes, cache_nsa_kv[e], cache_fox_kv[e], cache_fox_logf[e],
                                    state_nsa_win[e], page_table)
            even_s.append(outs)
        else:
            o = i // 2
            w_in = _prep_odd_w_in(w_in_odd[o])
            w_out = w_out_odd[o].astype(BF16)
            hp, tail = _odd_prompt(hp, b, s, g, w_in, w_out, swa_sink[o], swa_slopes)
            odd_p.append(tail)
            hs, new_win = _odd_sample(hs, g, w_in, w_out, swa_sink[o], swa_slopes, state_swa_win[o])
            odd_s.append(new_win)
        ffn_w = (w_ffn_up[i].astype(BF16), w_ffn_down[i].astype(BF16),
                 w_ple_proj[i].astype(BF16), w_ple_gate[i].astype(BF16))
        hp = _ffn_and_ple(hp, p_prompt[i].reshape(b * s, -1), g, *ffn_w)
        hs = _ffn_and_ple(hs, p_sample[i].reshape(db, -1), g, *ffn_w)
    stack = lambda seq, k: jnp.stack([t[k] for t in seq])
    return (hp.reshape(b, s, d), hs.reshape(db, 1, d),
            stack(even_p, 0), stack(even_s, 0), stack(even_p, 1), stack(even_s, 1),
            stack(even_p, 2), stack(even_s, 2), stack(even_p, 3), stack(even_s, 3),
            jnp.stack(odd_p), jnp.stack(odd_s))
```

```python
import functools

import jax
import jax.numpy as jnp
from jax import lax
from jax.experimental import pallas as pl
from jax.experimental.pallas import tpu as pltpu

F32 = jnp.float32
BF16 = jnp.bfloat16

HEAD_DIM = 128
NSA_HEADS = 8
NSA_KV = 2
NSA_REP = NSA_HEADS // NSA_KV
NSA_BLOCK = 64
NSA_BLOCK_SHIFT = 6
NSA_N_SEL = 16
NSA_WINDOW = 512
NSA_FORCE = 1e4
FOX_HEADS = 8
SWA_HEADS = 32
SWA_KV = 4
SWA_REP = SWA_HEADS // SWA_KV
SWA_HEAD_DIM = 64
SWA_WINDOW = 128
PAGE_SIZE = 128
EPS = 1e-6

LANES = 128
MASKED = -1e30
LOG2E = 1.4426950408889634
LOGIT_CHUNK_VREGS = 128
VMEM_LIMIT = 56 * 1024 * 1024

E_QA = 0
E_KVA = NSA_HEADS * HEAD_DIM
E_QB = E_KVA + 6 * NSA_KV * HEAD_DIM
E_KB = E_QB + FOX_HEADS * HEAD_DIM
E_VB = E_KB + FOX_HEADS * HEAD_DIM
E_GATE = E_VB + FOX_HEADS * HEAD_DIM
E_FORGET_LANE = 3 * NSA_HEADS
E_COLS = 6144
O_Q = 0
O_K = SWA_HEADS * SWA_HEAD_DIM
O_V = O_K + SWA_KV * LANES
O_COLS = O_V + SWA_KV * LANES


def _cparams(sem):
    return pltpu.CompilerParams(dimension_semantics=sem, vmem_limit_bytes=VMEM_LIMIT)


def _pick(n, pref):
    t = min(n, pref)
    while n % t:
        t //= 2
    return t


def _rms(x, g):
    return x * lax.rsqrt(jnp.mean(x * x, axis=-1, keepdims=True) + EPS) * g


def _split3(x):
    hi = x.astype(BF16)
    r = x - hi.astype(F32)
    mid = r.astype(BF16)
    lo = (r - mid.astype(F32)).astype(BF16)
    return hi, mid, lo


def _prenorm_mm_kernel(x_ref, g_ref, w_ref, o_ref, xn_ref):
    @pl.when(pl.program_id(1) == 0)
    def _():
        xn_ref[...] = _rms(x_ref[...], g_ref[...]).astype(BF16)

    o_ref[...] = jnp.dot(xn_ref[...], w_ref[...], preferred_element_type=F32)


def prenorm_matmul(x, g, w, *, tm=512, tn=1024):
    m, k = x.shape
    n = w.shape[1]
    tm, tn = _pick(m, tm), _pick(n, tn)
    return pl.pallas_call(
        _prenorm_mm_kernel,
        out_shape=jax.ShapeDtypeStruct((m, n), F32),
        grid=(m // tm, n // tn),
        in_specs=[pl.BlockSpec((tm, k), lambda i, j: (i, 0)),
                  pl.BlockSpec((1, k), lambda i, j: (0, 0)),
                  pl.BlockSpec((k, tn), lambda i, j: (0, j))],
        out_specs=pl.BlockSpec((tm, tn), lambda i, j: (i, j)),
        scratch_shapes=[pltpu.VMEM((tm, k), BF16)],
        compiler_params=_cparams(("parallel", "arbitrary")),
        name="prenorm_matmul",
    )(x, g.reshape(1, k), w)


def _prenorm_swiglu_kernel(x_ref, g_ref, wg_ref, wu_ref, o_ref, xn_ref):
    @pl.when(pl.program_id(1) == 0)
    def _():
        xn_ref[...] = _rms(x_ref[...], g_ref[...]).astype(BF16)

    xn = xn_ref[...]
    gate = jnp.dot(xn, wg_ref[...], preferred_element_type=F32)
    up = jnp.dot(xn, wu_ref[...], preferred_element_type=F32)
    o_ref[...] = (gate * jax.nn.sigmoid(gate) * up).astype(BF16)


def prenorm_swiglu(x, g, w_up, *, tm=512, tn=512):
    m, k = x.shape
    f = w_up.shape[1] // 2
    tm, tn = _pick(m, tm), _pick(f, tn)
    nj = f // tn
    return pl.pallas_call(
        _prenorm_swiglu_kernel,
        out_shape=jax.ShapeDtypeStruct((m, f), BF16),
        grid=(m // tm, nj),
        in_specs=[pl.BlockSpec((tm, k), lambda i, j: (i, 0)),
                  pl.BlockSpec((1, k), lambda i, j: (0, 0)),
                  pl.BlockSpec((k, tn), lambda i, j: (0, j)),
                  pl.BlockSpec((k, tn), lambda i, j: (0, j + nj))],
        out_specs=pl.BlockSpec((tm, tn), lambda i, j: (i, j)),
        scratch_shapes=[pltpu.VMEM((tm, k), BF16)],
        compiler_params=_cparams(("parallel", "arbitrary")),
        name="prenorm_swiglu",
    )(x, g.reshape(1, k), w_up, w_up)


def _mmn_postnorm_res_kernel(*refs, k_sizes):
    n_in = len(k_sizes)
    a_refs, (w_ref, g_ref, h_ref, o_ref) = refs[:n_in], refs[n_in:]
    acc = None
    off = 0
    for a_ref, ks in zip(a_refs, k_sizes):
        part = jnp.dot(a_ref[...].astype(BF16), w_ref[off:off + ks, :], preferred_element_type=F32)
        acc = part if acc is None else acc + part
        off += ks
    o_ref[...] = h_ref[...] + _rms(acc, g_ref[...])


def concat_matmul_postnorm_residual(a_list, w, g, h, *, tm=256):
    m, n = h.shape
    k_sizes = tuple(a.shape[1] for a in a_list)
    tm = _pick(m, tm)
    return pl.pallas_call(
        functools.partial(_mmn_postnorm_res_kernel, k_sizes=k_sizes),
        out_shape=jax.ShapeDtypeStruct((m, n), F32),
        grid=(m // tm,),
        in_specs=[pl.BlockSpec((tm, ks), lambda i: (i, 0)) for ks in k_sizes]
        + [pl.BlockSpec(w.shape, lambda i: (0, 0)),
           pl.BlockSpec((1, n), lambda i: (0, 0)),
           pl.BlockSpec((tm, n), lambda i: (i, 0))],
        out_specs=pl.BlockSpec((tm, n), lambda i: (i, 0)),
        compiler_params=_cparams(("parallel",)),
        name="concat_matmul_postnorm_residual",
    )(*a_list, w, g.reshape(1, n), h)


def _lane_pick(x, lane):
    idx = lax.broadcasted_iota(jnp.int32, x.shape, x.ndim - 1)
    return jnp.sum(jnp.where(idx == lane, x, 0.0), axis=-1, keepdims=True)


def _nsa_cmp_kernel(slopes_ref, q_ref, ksrc_ref, vsrc_ref, pe_ref, phi_ref, gz_ref, gb_ref,
                    o_ref, sel_ref, kc_ref, vc_ref, *, tq, nb, nbp):
    g = pl.program_id(1)
    qi = pl.program_id(2)

    @pl.when(qi == 0)
    def _():
        for src, dst, c in ((ksrc_ref, kc_ref, 0), (vsrc_ref, vc_ref, 1)):
            x = src[...].reshape(nb, NSA_BLOCK, HEAD_DIM) + pe_ref[c][None]
            pooled = jnp.mean(x, axis=1)
            dst[...] = jnp.zeros_like(dst)
            dst[0:nb, :] = jnp.dot(pooled.astype(BF16), phi_ref[c].astype(BF16),
                                   preferred_element_type=F32)

    q = q_ref[...]
    q_pos = qi * tq + lax.broadcasted_iota(jnp.int32, (tq, nbp), 0)
    n_idx = lax.broadcasted_iota(jnp.int32, (tq, nbp), 1)
    complete = (n_idx + 1) * NSA_BLOCK - 1 <= q_pos
    valid = complete & (n_idx < nb)
    dist = q_pos.astype(F32) - (n_idx.astype(F32) * NSA_BLOCK + (NSA_BLOCK - 1) / 2)
    kcb = kc_ref[...].astype(BF16)
    vcb = vc_ref[...].astype(BF16)
    gates = jax.nn.sigmoid(gz_ref[...] + gb_ref[...])
    scale = HEAD_DIM ** -0.5
    imp = jnp.zeros((tq, nbp), F32)
    for r in range(NSA_REP):
        qr = q[:, r * HEAD_DIM:(r + 1) * HEAD_DIM].astype(BF16)
        s = lax.dot_general(qr, kcb, (((1,), (1,)), ((), ())), preferred_element_type=F32) * scale
        s = s - slopes_ref[g * NSA_REP + r] * dist
        s = jnp.where(valid, s, MASKED)
        m = jnp.max(s, axis=-1, keepdims=True)
        m = jnp.where(m > 0.5 * MASKED, m, 0.0)
        e = jnp.exp(s - m)
        p = e / jnp.maximum(jnp.sum(e, axis=-1, keepdims=True), 1e-30)
        imp = imp + p
        o_r = jnp.dot(p.astype(BF16), vcb, preferred_element_type=F32)
        o_ref[:, r * HEAD_DIM:(r + 1) * HEAD_DIM] = _lane_pick(gates, g * NSA_REP + r) * o_r

    cur = jnp.right_shift(q_pos, NSA_BLOCK_SHIFT)
    forced = (n_idx == 0) | (n_idx == cur) | (n_idx == cur - 1)
    score = jnp.where(forced, NSA_FORCE, jnp.where(complete, imp, -NSA_FORCE))
    score = jnp.where(n_idx < nb, score, -jnp.inf)
    rank = jnp.zeros((tq, nbp), F32)
    for i in range(nb):
        ci = score[:, i:i + 1]
        rank = rank + jnp.where((ci > score) | ((ci == score) & (n_idx > i)), 1.0, 0.0)
    sel_ref[...] = jnp.where((rank < min(NSA_N_SEL, nb)) & (n_idx < nb), 1.0, 0.0)


def nsa_compress_select(z, slopes, pe, phi, gate_b, *, tq=256):
    b, s, _ = z.shape
    nb = s // NSA_BLOCK
    assert s % NSA_BLOCK == 0 and nb % 8 == 0 and nb <= LANES
    tq = _pick(s, tq)
    cb = lambda off: off // HEAD_DIM
    kern = functools.partial(_nsa_cmp_kernel, tq=tq, nb=nb, nbp=LANES)
    return pl.pallas_call(
        kern,
        out_shape=(jax.ShapeDtypeStruct((b, s, NSA_HEADS * HEAD_DIM), F32),
                   jax.ShapeDtypeStruct((b, NSA_KV, s, LANES), F32)),
        grid=(b, NSA_KV, s // tq),
        in_specs=[pl.BlockSpec(memory_space=pltpu.SMEM),
                  pl.BlockSpec((None, tq, NSA_REP * HEAD_DIM), lambda bi, g, i: (bi, i, g)),
                  pl.BlockSpec((None, s, HEAD_DIM), lambda bi, g, i: (bi, 0, cb(E_KVA) + g)),
                  pl.BlockSpec((None, s, HEAD_DIM), lambda bi, g, i: (bi, 0, cb(E_KVA) + NSA_KV + g)),
                  pl.BlockSpec(pe.shape, lambda bi, g, i: (0, 0, 0)),
                  pl.BlockSpec(phi.shape, lambda bi, g, i: (0, 0, 0)),
                  pl.BlockSpec((None, tq, LANES), lambda bi, g, i: (bi, i, cb(E_GATE))),
                  pl.BlockSpec((1, LANES), lambda bi, g, i: (0, 0))],
        out_specs=(pl.BlockSpec((None, tq, NSA_REP * HEAD_DIM), lambda bi, g, i: (bi, i, g)),
                   pl.BlockSpec((None, None, tq, LANES), lambda bi, g, i: (bi, g, i, 0))),
        scratch_shapes=[pltpu.VMEM((LANES, HEAD_DIM), F32), pltpu.VMEM((LANES, HEAD_DIM), F32)],
        compiler_params=_cparams(("parallel", "parallel", "arbitrary")),
        name="nsa_compress_select",
    )(slopes, z, z, z, pe, phi, z, gate_b)


def _kv_tile(qi, j, *, tq, tk, window):
    q0 = qi * tq
    first = 0 if window is None else lax.div(jnp.maximum(q0 - window, 0), tk)
    last = lax.div(q0 + tq - 1, tk)
    return first + j, last


def _flash_kernel(*refs, mode, tq, tk, rep, nsteps, window, branch, rc):
    it = iter(refs)
    slopes_ref = next(it) if mode in ("sel", "win", "swa") else None
    sink_ref = next(it) if mode == "swa" else None
    q_ref, k_ref, v_ref = next(it), next(it), next(it)
    sel_ref = next(it) if mode == "sel" else None
    fq_ref, fk_ref = (next(it), next(it)) if mode == "fox" else (None, None)
    prev_ref, gz_ref, gb_ref = (next(it), next(it), next(it)) if mode in ("sel", "win") else (None,) * 3
    o_ref, m_ref, l_ref, acc_ref = next(it), next(it), next(it), next(it)

    g = pl.program_id(1)
    qi = pl.program_id(2)
    j = pl.program_id(3)
    kj, kv_last = _kv_tile(qi, j, tq=tq, tk=tk, window=window)
    head_dim = SWA_HEAD_DIM if mode == "swa" else HEAD_DIM
    qscale = head_dim ** -0.5 * LOG2E
    q0 = qi * tq
    n_chunks = tq // rc
    row_in_chunk = lax.broadcasted_iota(jnp.int32, (rc, 1), 0)

    @pl.when(j == 0)
    def _():
        for r in range(rep):
            if mode == "swa":
                slope = slopes_ref[g * rep + r] * LOG2E
                for c in range(n_chunks):
                    rows = slice(c * rc, (c + 1) * rc)
                    m_ref[r, rows] = sink_ref[g * rep + r] * LOG2E + slope * row_in_chunk.astype(F32)
                l_ref[r] = jnp.ones((tq, 1), F32)
            else:
                m_ref[r] = jnp.full((tq, 1), MASKED, F32)
                l_ref[r] = jnp.zeros((tq, 1), F32)
        acc_ref[...] = jnp.zeros_like(acc_ref)

    @pl.when(kj <= kv_last)
    def _():
        k = k_ref[...].astype(BF16)
        v = v_ref[...].astype(BF16)
        k0 = kj * tk
        col = lax.broadcasted_iota(jnp.int32, (1, tk), 1)
        diff = lax.broadcasted_iota(jnp.int32, (rc, tk), 0) - lax.broadcasted_iota(jnp.int32, (rc, tk), 1)
        if mode == "sel":
            blk_of_key = jnp.right_shift(k0 + lax.broadcasted_iota(jnp.int32, (tk, LANES), 0),
                                         NSA_BLOCK_SHIFT)
            onehot = jnp.where(blk_of_key == lax.broadcasted_iota(jnp.int32, (tk, LANES), 1), 1.0, 0.0)
            k = jnp.concatenate([k, onehot.astype(BF16)], axis=1)
            penalty = jnp.where(sel_ref[...] > 0.5, 0.0, MASKED).astype(BF16)
        if mode == "fox":
            f_keys = fk_ref[...]
        def scores(r, c):
            rows = slice(c * rc, (c + 1) * rc)
            if mode == "swa":
                qp = q_ref[rows, (r // 2) * LANES:(r // 2 + 1) * LANES]
                upper = lax.broadcasted_iota(jnp.int32, qp.shape, 1) >= SWA_HEAD_DIM
                qr = jnp.where(upper if r % 2 else ~upper, qp, 0.0)
            else:
                qr = q_ref[rows, r * HEAD_DIM:(r + 1) * HEAD_DIM]
            qr = (qr * qscale).astype(BF16)
            if mode == "sel":
                qr = jnp.concatenate([qr, penalty[rows]], axis=1)
            return lax.dot_general(qr, k, (((1,), (1,)), ((), ())), preferred_element_type=F32)

        def update(r, c, s):
            rows = slice(c * rc, (c + 1) * rc)
            shift = q0 + c * rc - k0
            if mode == "fox":
                s = s + (fq_ref[:, c * rc:c * rc + 1] - f_keys) * LOG2E
            else:
                s = s + (slopes_ref[g * rep + r] * LOG2E) * (col - shift).astype(F32)
            mask = diff >= -shift
            if window is not None:
                mask = mask & (diff <= window - shift)
            s = jnp.where(mask, s, MASKED)
            m_prev = m_ref[r, rows]
            m_new = jnp.maximum(m_prev, jnp.max(s, axis=-1, keepdims=True))
            alpha = jnp.exp2(m_prev - m_new)
            p = jnp.exp2(s - m_new)
            l_ref[r, rows] = alpha * l_ref[r, rows] + jnp.sum(p, axis=-1, keepdims=True)
            acc_ref[r, rows] = alpha * acc_ref[r, rows] + jnp.dot(p.astype(BF16), v,
                                                                  preferred_element_type=F32)
            m_ref[r, rows] = m_new

        for r in range(rep):
            for c in range(n_chunks):
                update(r, c, scores(r, c))

    @pl.when(j == nsteps - 1)
    def _():
        if mode in ("sel", "win"):
            gates = jax.nn.sigmoid(gz_ref[...] + gb_ref[...])
        for r in range(rep):
            o = acc_ref[r] / l_ref[r]
            blk = slice(r * HEAD_DIM, (r + 1) * HEAD_DIM)
            if mode in ("sel", "win"):
                gate = _lane_pick(gates, branch * NSA_HEADS + g * rep + r)
                o_ref[:, blk] = prev_ref[:, blk] + gate * o
            elif mode == "fox":
                o_ref[...] = o
            elif r % 2:
                o_even = acc_ref[r - 1] / l_ref[r - 1]
                upper = lax.broadcasted_iota(jnp.int32, o.shape, 1) >= SWA_HEAD_DIM
                o_ref[:, (r // 2) * LANES:(r // 2 + 1) * LANES] = jnp.where(upper, o, o_even)


def _flash_call(mode, z, *, tq, tk, rep, n_kv, window, branch, q_col, k_col, v_col, out_cols,
                smem=(), extra=(), extra_specs=()):
    b, s, _ = z.shape
    tq, tk = _pick(s, tq), _pick(s, tk)
    nq = s // tq

    def first_last(qi):
        first = 0 if window is None else max(qi * tq - window, 0) // tk
        return first, (qi * tq + tq - 1) // tk

    nsteps = max(l - f + 1 for f, l in map(first_last, range(nq)))
    qw = rep * HEAD_DIM if mode != "swa" else rep * SWA_HEAD_DIM

    def kv_map(col):
        def index(bi, g, i, j):
            kj, last = _kv_tile(i, j, tq=tq, tk=tk, window=window)
            return bi, jnp.minimum(kj, last), col + g
        return index

    in_specs = [pl.BlockSpec(memory_space=pltpu.SMEM) for _ in smem]
    in_specs += [pl.BlockSpec((None, tq, qw), lambda bi, g, i, j: (bi, i, q_col + g)),
                 pl.BlockSpec((None, tk, LANES), kv_map(k_col)),
                 pl.BlockSpec((None, tk, LANES), kv_map(v_col))]
    in_specs += list(extra_specs(tq, tk, kv_map))
    rc = min(tq, max(8, LOGIT_CHUNK_VREGS * 8 * LANES // tk))
    kern = functools.partial(_flash_kernel, mode=mode, tq=tq, tk=tk, rep=rep, nsteps=nsteps,
                             window=window, branch=branch, rc=rc)
    return pl.pallas_call(
        kern,
        out_shape=jax.ShapeDtypeStruct((b, s, out_cols), F32),
        grid=(b, n_kv, nq, nsteps),
        in_specs=in_specs,
        out_specs=pl.BlockSpec((None, tq, qw), lambda bi, g, i, j: (bi, i, g)),
        scratch_shapes=[pltpu.VMEM((rep, tq, 1), F32), pltpu.VMEM((rep, tq, 1), F32),
                        pltpu.VMEM((rep, tq, LANES), F32)],
        compiler_params=_cparams(("parallel", "parallel", "parallel", "arbitrary")),
        name="flash_" + mode,
    )(*smem, z, z, z, *extra)


def _nsa_gate_specs(tq):
    return [pl.BlockSpec((None, tq, NSA_REP * HEAD_DIM), lambda bi, g, i, j: (bi, i, g)),
            pl.BlockSpec((None, tq, LANES), lambda bi, g, i, j: (bi, i, E_GATE // LANES)),
            pl.BlockSpec((1, LANES), lambda bi, g, i, j: (0, 0))]


def nsa_selected(z, sel, prev, slopes, gate_b, *, t=512):
    def specs(tq, tk, kv_map):
        return [pl.BlockSpec((None, None, tq, LANES), lambda bi, g, i, j: (bi, g, i, 0))] + _nsa_gate_specs(tq)
    kvc = E_KVA // HEAD_DIM
    return _flash_call("sel", z, tq=t, tk=t, rep=NSA_REP, n_kv=NSA_KV, window=None, branch=1,
                       q_col=0, k_col=kvc + 2 * NSA_KV, v_col=kvc + 3 * NSA_KV,
                       out_cols=NSA_HEADS * HEAD_DIM, smem=(slopes,), extra=(sel, prev, z, gate_b),
                       extra_specs=specs)


def nsa_window(z, prev, slopes, gate_b, *, t=512):
    def specs(tq, tk, kv_map):
        return _nsa_gate_specs(tq)
    kvc = E_KVA // HEAD_DIM
    return _flash_call("win", z, tq=t, tk=t, rep=NSA_REP, n_kv=NSA_KV, window=NSA_WINDOW, branch=2,
                       q_col=0, k_col=kvc + 4 * NSA_KV, v_col=kvc + 5 * NSA_KV,
                       out_cols=NSA_HEADS * HEAD_DIM, smem=(slopes,), extra=(prev, z, gate_b),
                       extra_specs=specs)


def fox_attention(z, f_cum_t, *, t=512):
    def specs(tq, tk, kv_map):
        def fk_map(bi, g, i, j):
            return (bi, g, 0, kv_map(0)(bi, g, i, j)[1])
        return [pl.BlockSpec((None, None, 1, tq), lambda bi, g, i, j: (bi, g, 0, i)),
                pl.BlockSpec((None, None, 1, tk), fk_map)]
    return _flash_call("fox", z, tq=t, tk=t, rep=1, n_kv=FOX_HEADS, window=None, branch=0,
                       q_col=E_QB // HEAD_DIM, k_col=E_KB // HEAD_DIM, v_col=E_VB // HEAD_DIM,
                       out_cols=FOX_HEADS * HEAD_DIM, extra=(f_cum_t, f_cum_t), extra_specs=specs)


def swa_attention(z, slopes, sink, *, t=256):
    return _flash_call("swa", z, tq=t, tk=t, rep=SWA_REP, n_kv=SWA_KV, window=SWA_WINDOW, branch=0,
                       q_col=0, k_col=O_K // LANES, v_col=O_V // LANES,
                       out_cols=SWA_HEADS * SWA_HEAD_DIM, smem=(slopes, sink),
                       extra_specs=lambda tq, tk, kv_map: [])


def _fox_gates_kernel(gz_ref, fb_ref, logf_ref, f_ref, ft_ref, *, chunk):
    x = gz_ref[...] + fb_ref[...]
    logf = jnp.minimum(x, 0.0) - jnp.log1p(jnp.exp(-jnp.abs(x)))
    logf_ref[...] = logf
    tri = jnp.where(lax.broadcasted_iota(jnp.int32, (chunk, chunk), 0)
                    >= lax.broadcasted_iota(jnp.int32, (chunk, chunk), 1), 1.0, 0.0).astype(BF16)
    carry = jnp.zeros((1, LANES), F32)
    for c in range(logf.shape[0] // chunk):
        rows = slice(c * chunk, (c + 1) * chunk)
        y = carry
        for part in _split3(logf[rows]):
            y = y + jnp.dot(tri, part, preferred_element_type=F32)
        f_ref[rows, :] = y
        carry = y[chunk - 1:chunk, :]
    ft_ref[...] = f_ref[...].T


def fox_gates(z, forget_b_lanes, *, chunk=256):
    b, s, _ = z.shape
    chunk = _pick(s, chunk)
    shp = jax.ShapeDtypeStruct((b, s, LANES), F32)
    return pl.pallas_call(
        functools.partial(_fox_gates_kernel, chunk=chunk),
        out_shape=(shp, shp, jax.ShapeDtypeStruct((b, LANES, s), F32)),
        grid=(b,),
        in_specs=[pl.BlockSpec((None, s, LANES), lambda bi: (bi, 0, E_GATE // LANES)),
                  pl.BlockSpec((1, LANES), lambda bi: (0, 0))],
        out_specs=(pl.BlockSpec((None, s, LANES), lambda bi: (bi, 0, 0)),
                   pl.BlockSpec((None, s, LANES), lambda bi: (bi, 0, 0)),
                   pl.BlockSpec((None, LANES, s), lambda bi: (bi, 0, 0))),
        compiler_params=_cparams(("parallel",)),
        name="fox_gates",
    )(z, forget_b_lanes)


def _alibi_slopes(n):
    return 2.0 ** (-8.0 * jnp.arange(1, n + 1, dtype=F32) / n)


def _prep_even_w_in(w):
    d = w.shape[0]
    n_gate = 3 * NSA_HEADS
    src_gate = E_QB
    src_qb = src_gate + n_gate
    src_forget = src_qb + 3 * FOX_HEADS * HEAD_DIM
    used = E_GATE + n_gate + FOX_HEADS
    return jnp.concatenate([w[:, :src_gate], w[:, src_qb:src_forget], w[:, src_gate:src_qb],
                            w[:, src_forget:], jnp.zeros((d, E_COLS - used), w.dtype)], axis=1).astype(BF16)


def _prep_odd_w_in(w):
    d = w.shape[0]
    kw = SWA_KV * SWA_HEAD_DIM
    k = w[:, O_K:O_K + kw].reshape(d, SWA_KV, SWA_HEAD_DIM)
    v = w[:, O_K + kw:].reshape(d, SWA_KV, SWA_HEAD_DIM)
    dup = lambda x: jnp.concatenate([x, x], axis=-1).reshape(d, SWA_KV * LANES)
    return jnp.concatenate([w[:, :O_K], dup(k), dup(v)], axis=1).astype(BF16)


def _lane_row(values, offset):
    return jnp.zeros((1, LANES), F32).at[0, offset:offset + values.shape[0]].set(values)


def _ffn_down_ple_kernel(a_ref, wd_ref, g3_ref, h_ref, p_ref, wpp_ref, wpg_ref, g4_ref, o_ref):
    y = jnp.dot(a_ref[...], wd_ref[...], preferred_element_type=F32)
    h = h_ref[...] + _rms(y, g3_ref[...])
    proj = jnp.dot(p_ref[...].astype(BF16), wpp_ref[...], preferred_element_type=F32)
    gate = jnp.dot(h.astype(BF16), wpg_ref[...], preferred_element_type=F32)
    o_ref[...] = h + _rms(proj, g4_ref[...]) * jax.nn.sigmoid(gate)


def ffn_down_ple(act, w_down, g3, h, p, w_pp, w_pg, g4, *, tm=256):
    m, d = h.shape
    f = act.shape[1]
    pd = p.shape[1]
    tm = _pick(m, tm)
    row = lambda width: pl.BlockSpec((tm, width), lambda i: (i, 0))
    whole = lambda shape: pl.BlockSpec(shape, lambda i: (0, 0))
    return pl.pallas_call(
        _ffn_down_ple_kernel,
        out_shape=jax.ShapeDtypeStruct((m, d), F32),
        grid=(m // tm,),
        in_specs=[row(f), whole((f, d)), whole((1, d)), row(d), row(pd), whole((pd, d)), whole((d, d)),
                  whole((1, d))],
        out_specs=row(d),
        compiler_params=_cparams(("parallel",)),
        name="ffn_down_ple",
    )(act, w_down, g3.reshape(1, d), h, p, w_pp, w_pg, g4.reshape(1, d))


def _ffn_and_ple(h, p, g, w_up, w_down, w_pp, w_pg):
    act = prenorm_swiglu(h, g[2], w_up)
    return ffn_down_ple(act, w_down, g[3], h, p, w_pp, w_pg, g[4])


def _even_prompt(h, b, s, g, w_in, w_out, gate_b, forget_b, pe, phi, slopes):
    z = prenorm_matmul(h, g[0], w_in).reshape(b, s, E_COLS)
    o_a, sel = nsa_compress_select(z, slopes, pe, phi, gate_b)
    o_a = nsa_selected(z, sel, o_a, slopes, gate_b)
    o_a = nsa_window(z, o_a, slopes, gate_b)
    logf, f_cum, f_cum_t = fox_gates(z, forget_b)
    fl = slice(E_FORGET_LANE, E_FORGET_LANE + FOX_HEADS)
    o_b = fox_attention(z, f_cum_t[:, fl, :].reshape(b, FOX_HEADS, 1, s))
    m = b * s
    h = concat_matmul_postnorm_residual([o_a.reshape(m, -1), o_b.reshape(m, -1)], w_out, g[1], h)
    wb = min(NSA_WINDOW, s)
    win_off = E_KVA + 4 * NSA_KV * HEAD_DIM
    outs = (z[:, :, E_KVA:win_off].reshape(b, s, 4, NSA_KV, HEAD_DIM),
            z[:, :, E_KB:E_GATE].reshape(b, s, 2, FOX_HEADS, HEAD_DIM),
            logf[:, :, fl],
            z[:, s - wb:, win_off:E_QB].reshape(b, wb, 2, NSA_KV, HEAD_DIM))
    return h, outs


def _odd_prompt(h, b, s, g, w_in, w_out, sink, slopes):
    z = prenorm_matmul(h, g[0], w_in).reshape(b, s, O_COLS)
    o = swa_attention(z, slopes, sink)
    h = concat_matmul_postnorm_residual([o.reshape(b * s, -1)], w_out, g[1], h)
    wb = min(SWA_WINDOW, s)
    tail = z[:, s - wb:, O_K:].reshape(b, wb, 2, SWA_KV, 2, SWA_HEAD_DIM)[:, :, :, :, 0]
    return h, tail


NSA_POOL_PAGES = 16
FOX_PAGES = 8


def _nsa_pool_kernel(pt_ref, *refs, n_pages):
    page_refs, (pe_ref, o_ref) = refs[:n_pages], refs[n_pages:]
    blocks_per_page = PAGE_SIZE // NSA_BLOCK
    for i, ref in enumerate(page_refs):
        x = ref[...].reshape(blocks_per_page, NSA_BLOCK, ref.shape[-1]) + pe_ref[...][None]
        o_ref[i * blocks_per_page:(i + 1) * blocks_per_page, :] = jnp.mean(x, axis=1)


def nsa_pool_pages(cache, page_table, pe_lanes):
    db, n_pages = page_table.shape
    width = 2 * NSA_KV * HEAD_DIM
    p = _pick(n_pages, NSA_POOL_PAGES)
    bpp = PAGE_SIZE // NSA_BLOCK

    def page_map(i):
        return lambda b, c, pt: (pt[b, c * p + i], 0, 0)

    return pl.pallas_call(
        functools.partial(_nsa_pool_kernel, n_pages=p),
        out_shape=jax.ShapeDtypeStruct((db, n_pages * bpp, width), F32),
        grid_spec=pltpu.PrefetchScalarGridSpec(
            num_scalar_prefetch=1,
            grid=(db, n_pages // p),
            in_specs=[pl.BlockSpec((None, PAGE_SIZE, width), page_map(i)) for i in range(p)]
            + [pl.BlockSpec((NSA_BLOCK, width), lambda b, c, pt: (0, 0))],
            out_specs=pl.BlockSpec((None, p * bpp, width), lambda b, c, pt: (b, c, 0))),
        compiler_params=_cparams(("parallel", "arbitrary")),
        name="nsa_pool_pages",
    )(page_table, *([cache] * p), pe_lanes)


def _transposed_rows(row):
    n = row.shape[1]
    bc = jnp.broadcast_to(row, (n, n))
    return bc.T, bc


def _nsa_sample_cmp_kernel(q_ref, pooled_ref, phi_ref, slopes_ref, gz_ref, gb_ref, o_ref, idx_ref,
                           *, q_pos, n_pick):
    npb = pooled_ref.shape[0]
    q = q_ref[...].astype(BF16)
    pooled = pooled_ref[...].astype(BF16)
    n_row = lax.broadcasted_iota(jnp.int32, (1, npb), 1)
    complete = (n_row + 1) * NSA_BLOCK - 1 <= q_pos
    dist = q_pos - (n_row.astype(F32) * NSA_BLOCK + (NSA_BLOCK - 1) / 2)
    head_row = lax.broadcasted_iota(jnp.int32, (NSA_HEADS, 1), 0)
    gates = jax.nn.sigmoid(gz_ref[...] + gb_ref[...])
    lane = lax.broadcasted_iota(jnp.int32, (NSA_HEADS, LANES), 1)
    gate_cmp = jnp.sum(jnp.where(lane == head_row, gates, 0.0), axis=-1, keepdims=True)
    scale = HEAD_DIM ** -0.5
    i_idx = lax.broadcasted_iota(jnp.int32, (npb, npb), 0)
    j_idx = lax.broadcasted_iota(jnp.int32, (npb, npb), 1)
    o_acc = jnp.zeros((NSA_HEADS, HEAD_DIM), F32)
    for g in range(NSA_KV):
        in_group = (head_row >= g * NSA_REP) & (head_row < (g + 1) * NSA_REP)
        kc = jnp.dot(pooled[:, g * HEAD_DIM:(g + 1) * HEAD_DIM], phi_ref[0].astype(BF16),
                     preferred_element_type=F32).astype(BF16)
        vc = jnp.dot(pooled[:, (NSA_KV + g) * HEAD_DIM:(NSA_KV + g + 1) * HEAD_DIM],
                     phi_ref[1].astype(BF16), preferred_element_type=F32).astype(BF16)
        s = lax.dot_general(q, kc, (((1,), (1,)), ((), ())), preferred_element_type=F32) * scale
        s = s - slopes_ref[...] * dist
        s = jnp.where(complete, s, MASKED)
        m = jnp.max(s, axis=-1, keepdims=True)
        m = jnp.where(m > 0.5 * MASKED, m, 0.0)
        e = jnp.exp(s - m)
        p = e / jnp.maximum(jnp.sum(e, axis=-1, keepdims=True), 1e-30)
        o_g = jnp.dot(p.astype(BF16), vc, preferred_element_type=F32)
        o_acc = jnp.where(in_group, o_g, o_acc)
        imp = jnp.sum(jnp.where(in_group, p, 0.0), axis=0, keepdims=True)

        cur = q_pos // NSA_BLOCK
        forced = (n_row == 0) | (n_row == cur) | (n_row == cur - 1)
        score = jnp.where(forced, NSA_FORCE, jnp.where(complete, imp, -NSA_FORCE))
        col, bc = _transposed_rows(score)
        rank = jnp.sum(jnp.where((col > bc) | ((col == bc) & (i_idx < j_idx)), 1.0, 0.0),
                       axis=0, keepdims=True)
        sel = jnp.where(rank < n_pick, 1.0, 0.0)
        sel_col, _ = _transposed_rows(sel)
        pos = jnp.sum(jnp.where(i_idx < j_idx, sel_col, 0.0), axis=0, keepdims=True)
        slot = lax.broadcasted_iota(jnp.int32, (NSA_N_SEL, npb), 0).astype(F32)
        n_f = lax.broadcasted_iota(jnp.int32, (NSA_N_SEL, npb), 1).astype(F32)
        hit = (sel > 0.5) & (pos == slot)
        ids = jnp.sum(jnp.where(hit, n_f, 0.0), axis=-1, keepdims=True)
        idx_ref[g] = jnp.broadcast_to(ids, (NSA_N_SEL, LANES)).astype(jnp.int32)
    o_ref[...] = gate_cmp * o_acc


def nsa_sample_compress_select(q, pooled, phi, slopes, gz, gate_b, *, q_pos, n_pick):
    db = q.shape[0]
    npb = pooled.shape[1]
    kern = functools.partial(_nsa_sample_cmp_kernel, q_pos=q_pos, n_pick=n_pick)
    return pl.pallas_call(
        kern,
        out_shape=(jax.ShapeDtypeStruct((db, NSA_HEADS, HEAD_DIM), F32),
                   jax.ShapeDtypeStruct((db, NSA_KV, NSA_N_SEL, LANES), jnp.int32)),
        grid=(db,),
        in_specs=[pl.BlockSpec((None, NSA_HEADS, HEAD_DIM), lambda b: (b, 0, 0)),
                  pl.BlockSpec((None, npb, pooled.shape[2]), lambda b: (b, 0, 0)),
                  pl.BlockSpec(phi.shape, lambda b: (0, 0, 0)),
                  pl.BlockSpec((NSA_HEADS, 1), lambda b: (0, 0)),
                  pl.BlockSpec((None, 1, LANES), lambda b: (b, 0, 0)),
                  pl.BlockSpec((1, LANES), lambda b: (0, 0))],
        out_specs=(pl.BlockSpec((None, NSA_HEADS, HEAD_DIM), lambda b: (b, 0, 0)),
                   pl.BlockSpec((None, NSA_KV, NSA_N_SEL, LANES), lambda b: (b, 0, 0, 0))),
        compiler_params=_cparams(("parallel",)),
        name="nsa_sample_compress_select",
    )(q, pooled, phi, slopes.reshape(NSA_HEADS, 1), gz, gate_b)


def _nsa_sample_attend_kernel(idx_ref, pt_ref, *refs, n_pick, q_pos):
    n_blk = 2 * NSA_KV * n_pick
    blk_refs = refs[:n_blk]
    q_ref, new_ref, win_ref, prev_ref, slopes_ref, gz_ref, gb_ref, o_ref, win_out_ref = refs[n_blk:]
    b = pl.program_id(0)
    w = win_ref.shape[0]
    scale = HEAD_DIM ** -0.5
    gates = jax.nn.sigmoid(gz_ref[...] + gb_ref[...])
    r_col = lax.broadcasted_iota(jnp.int32, (NSA_REP, 1), 0)
    lane = lax.broadcasted_iota(jnp.int32, (NSA_REP, LANES), 1)
    row_w = lax.broadcasted_iota(jnp.int32, (w, HEAD_DIM), 0)
    n_keys = n_pick * NSA_BLOCK
    key_lane = lax.broadcasted_iota(jnp.int32, (1, n_keys), 1)

    def rounded(x):
        return x.astype(BF16).astype(F32)

    def attend(q4, keys, vals, k_new, v_new, dist, valid, slope):
        s = lax.dot_general(q4.astype(BF16), keys.astype(BF16), (((1,), (1,)), ((), ())),
                            preferred_element_type=F32) * scale - slope * dist
        s = jnp.where(valid, s, MASKED)
        s_new = jnp.sum(rounded(q4) * rounded(k_new), axis=-1, keepdims=True) * scale
        m = jnp.maximum(jnp.max(s, axis=-1, keepdims=True), s_new)
        e = jnp.exp(s - m)
        e_new = jnp.exp(s_new - m)
        den = jnp.sum(e, axis=-1, keepdims=True) + e_new
        num = jnp.dot(e.astype(BF16), vals.astype(BF16), preferred_element_type=F32)
        return (num + rounded(e_new) * rounded(v_new)) / den

    for g in range(NSA_KV):
        q4 = q_ref[g]
        slope = slopes_ref[g]
        new = lambda c: new_ref[c * NSA_KV + g:c * NSA_KV + g + 1, :]
        gate = lambda c: jnp.sum(jnp.where(lane == c * NSA_HEADS + g * NSA_REP + r_col, gates, 0.0),
                                 axis=-1, keepdims=True)
        kb = blk_refs[(2 * g) * n_pick:(2 * g + 1) * n_pick]
        vb = blk_refs[(2 * g + 1) * n_pick:(2 * g + 2) * n_pick]
        keys = jnp.concatenate([r[...] for r in kb], axis=0)
        vals = jnp.concatenate([r[...] for r in vb], axis=0)
        k_pos = key_lane & (NSA_BLOCK - 1)
        for i in range(n_pick):
            in_blk = jnp.right_shift(key_lane, NSA_BLOCK_SHIFT) == i
            k_pos = k_pos + jnp.where(in_blk, idx_ref[b, g, i] * NSA_BLOCK, 0)
        dist = q_pos - k_pos
        o_sel = attend(q4, keys, vals, new(2), new(3), dist.astype(F32), dist >= 0, slope)
        k_w = win_ref[:, g * HEAD_DIM:(g + 1) * HEAD_DIM]
        v_w = win_ref[:, (NSA_KV + g) * HEAD_DIM:(NSA_KV + g + 1) * HEAD_DIM]
        dist_w = w - lax.broadcasted_iota(jnp.int32, (1, w), 1)
        o_win = attend(q4, k_w, v_w, new(4), new(5), dist_w.astype(F32), dist_w <= NSA_WINDOW, slope)
        o_ref[g] = prev_ref[g] + gate(1) * o_sel + gate(2) * o_win
        for c, buf in ((0, k_w), (1, v_w)):
            rolled = pltpu.roll(buf, shift=w - 1, axis=0)
            col = (c * NSA_KV + g) * HEAD_DIM
            win_out_ref[:, col:col + HEAD_DIM] = jnp.where(row_w == w - 1, new(4 + c), rolled)


def nsa_sample_attend(cache, idx, page_table, q, new_rows, win, prev, slopes, gz, gate_b, *, n_pick, q_pos):
    db = q.shape[0]
    w = win.shape[1]
    bpp_shift = (PAGE_SIZE // NSA_BLOCK).bit_length() - 1

    def blk_map(g, i, comp):
        def index(b, idx_r, pt_r):
            blk = idx_r[b, g, i]
            return (pt_r[b, jnp.right_shift(blk, bpp_shift)], blk & (PAGE_SIZE // NSA_BLOCK - 1),
                    comp * NSA_KV + g)
        return index

    blk_specs = [pl.BlockSpec((None, NSA_BLOCK, HEAD_DIM), blk_map(g, i, comp))
                 for g in range(NSA_KV) for comp in (2, 3) for i in range(n_pick)]
    full = lambda shape: pl.BlockSpec(shape, lambda b, idx_r, pt_r: (0,) * len(shape))
    per_b = lambda shape: pl.BlockSpec((None,) + shape, lambda b, idx_r, pt_r: (b,) + (0,) * len(shape))
    kern = functools.partial(_nsa_sample_attend_kernel, n_pick=n_pick, q_pos=q_pos)
    return pl.pallas_call(
        kern,
        out_shape=(jax.ShapeDtypeStruct((db, NSA_KV, NSA_REP, HEAD_DIM), F32),
                   jax.ShapeDtypeStruct(win.shape, F32)),
        grid_spec=pltpu.PrefetchScalarGridSpec(
            num_scalar_prefetch=2,
            grid=(db,),
            in_specs=blk_specs + [per_b((NSA_KV, NSA_REP, HEAD_DIM)), per_b(new_rows.shape[1:]),
                                  per_b(win.shape[1:]), per_b((NSA_KV, NSA_REP, HEAD_DIM)),
                                  full((NSA_KV, NSA_REP, 1)), per_b((1, LANES)), full((1, LANES))],
            out_specs=(per_b((NSA_KV, NSA_REP, HEAD_DIM)), per_b(win.shape[1:]))),
        compiler_params=_cparams(("arbitrary",)),
        name="nsa_sample_attend",
    )(idx, page_table, *([cache] * len(blk_specs)), q, new_rows, win, prev,
      slopes.reshape(NSA_KV, NSA_REP, 1), gz, gate_b)


def _fox_decode_kernel(pt_ref, *refs, n_pages):
    kv_refs, lf_refs = refs[:n_pages], refs[n_pages:2 * n_pages]
    (q_ref, knew_ref, vnew_ref, vrow_ref, fraw_ref, fb_ref,
     o_ref, lfnew_ref, qbd_ref, m_ref, l_ref, acc_ref, carry_ref) = refs[2 * n_pages:]
    c = pl.program_id(1)
    width = FOX_HEADS * HEAD_DIM
    scale = HEAD_DIM ** -0.5
    head_row = lax.broadcasted_iota(jnp.int32, (FOX_HEADS, width), 0)
    head_of_lane = jnp.right_shift(lax.broadcasted_iota(jnp.int32, (FOX_HEADS, width), 1),
                                   HEAD_DIM.bit_length() - 1)

    @pl.when(c == 0)
    def _():
        q = q_ref[...]
        qbd_ref[...] = jnp.where(head_of_lane == head_row, jnp.concatenate([q] * FOX_HEADS, axis=1),
                                 0.0).astype(BF16)
        x = fraw_ref[...] + fb_ref[...]
        lf_new = jnp.minimum(x, 0.0) - jnp.log1p(jnp.exp(-jnp.abs(x)))
        lfnew_ref[...] = lf_new
        carry_ref[...] = lf_new
        rounded = lambda t: t.astype(BF16).astype(F32)
        m_ref[...] = jnp.sum(rounded(q) * rounded(knew_ref[...]), axis=-1, keepdims=True) * scale
        l_ref[...] = jnp.ones_like(l_ref)
        acc_ref[...] = jnp.broadcast_to(rounded(vrow_ref[...]), acc_ref.shape)

    later = jnp.where(lax.broadcasted_iota(jnp.int32, (PAGE_SIZE, PAGE_SIZE), 0)
                      > lax.broadcasted_iota(jnp.int32, (PAGE_SIZE, PAGE_SIZE), 1), 1.0, 0.0).astype(BF16)
    lf_all = jnp.concatenate([lf_ref[...] for lf_ref in lf_refs], axis=0)
    rows = lf_all.shape[0]
    suffix3 = jnp.dot(jnp.concatenate(_split3(lf_all), axis=0), later, preferred_element_type=F32)
    suffix = suffix3[:rows] + suffix3[rows:2 * rows] + suffix3[2 * rows:]
    totals = jnp.sum(lf_all, axis=-1, keepdims=True)
    carry = carry_ref[...]
    biases = []
    for i in range(n_pages):
        page_rows = slice(i * FOX_HEADS, (i + 1) * FOX_HEADS)
        biases.append(carry + suffix[page_rows])
        carry = carry + totals[page_rows]
    carry_ref[...] = carry
    bias = jnp.concatenate(biases, axis=1)
    kp = jnp.concatenate([kv_ref[:, :width].astype(BF16) for kv_ref in kv_refs], axis=0)
    vp = jnp.concatenate([kv_ref[:, width:].astype(BF16) for kv_ref in kv_refs], axis=0)
    s = lax.dot_general(qbd_ref[...], kp, (((1,), (1,)), ((), ())),
                        preferred_element_type=F32) * scale + bias
    m_prev = m_ref[...]
    m_new = jnp.maximum(m_prev, jnp.max(s, axis=-1, keepdims=True))
    alpha = jnp.exp(m_prev - m_new)
    p = jnp.exp(s - m_new)
    l_ref[...] = alpha * l_ref[...] + jnp.sum(p, axis=-1, keepdims=True)
    acc_ref[...] = alpha * acc_ref[...] + jnp.dot(p.astype(BF16), vp, preferred_element_type=F32)
    m_ref[...] = m_new

    @pl.when(c == pl.num_programs(1) - 1)
    def _():
        acc = jnp.where(head_of_lane == head_row, acc_ref[...], 0.0)
        o = acc[:, :HEAD_DIM]
        for h in range(1, FOX_HEADS):
            o = o + acc[:, h * HEAD_DIM:(h + 1) * HEAD_DIM]
        o_ref[...] = o / l_ref[...]


def fox_decode(cache_kv, cache_logf_t, page_table, q, k_new, v_new, f_raw, forget_b):
    db, n_pages = page_table.shape
    p = _pick(n_pages, FOX_PAGES)
    width = FOX_HEADS * HEAD_DIM

    def page_map(i):
        return lambda b, c, pt: (pt[b, n_pages - 1 - (c * p + i)], 0, 0)

    per_b = lambda shape: pl.BlockSpec((None,) + shape, lambda b, c, pt: (b,) + (0,) * len(shape))
    return pl.pallas_call(
        functools.partial(_fox_decode_kernel, n_pages=p),
        out_shape=(jax.ShapeDtypeStruct((db, FOX_HEADS, HEAD_DIM), F32),
                   jax.ShapeDtypeStruct((db, FOX_HEADS, 1), F32)),
        grid_spec=pltpu.PrefetchScalarGridSpec(
            num_scalar_prefetch=1,
            grid=(db, n_pages // p),
            in_specs=[pl.BlockSpec((None, PAGE_SIZE, 2 * width), page_map(i)) for i in range(p)]
            + [pl.BlockSpec((None, FOX_HEADS, PAGE_SIZE), page_map(i)) for i in range(p)]
            + [per_b((FOX_HEADS, HEAD_DIM)), per_b((FOX_HEADS, HEAD_DIM)), per_b((FOX_HEADS, HEAD_DIM)),
               per_b((1, width)), per_b((FOX_HEADS, 1)),
               pl.BlockSpec((FOX_HEADS, 1), lambda b, c, pt: (0, 0))],
            out_specs=(per_b((FOX_HEADS, HEAD_DIM)), per_b((FOX_HEADS, 1))),
            scratch_shapes=[pltpu.VMEM((FOX_HEADS, width), BF16), pltpu.VMEM((FOX_HEADS, 1), F32),
                            pltpu.VMEM((FOX_HEADS, 1), F32), pltpu.VMEM((FOX_HEADS, width), F32),
                            pltpu.VMEM((FOX_HEADS, 1), F32)]),
        compiler_params=_cparams(("parallel", "arbitrary")),
        name="fox_decode",
    )(page_table, *([cache_kv] * p), *([cache_logf_t] * p), q, k_new, v_new,
      v_new.reshape(db, 1, width), f_raw, forget_b.reshape(FOX_HEADS, 1))


def _swa_decode_kernel(q_ref, new_ref, win_ref, slopes_ref, sink_ref, o_ref, win_out_ref):
    w = win_ref.shape[0]
    kvw = SWA_KV * SWA_HEAD_DIM
    scale = SWA_HEAD_DIM ** -0.5
    rounded = lambda t: t.astype(BF16).astype(F32)
    dist = (w - lax.broadcasted_iota(jnp.int32, (1, w), 1))
    new = new_ref[...]
    for g in range(SWA_KV):
        q = q_ref[g]
        lo = g * SWA_HEAD_DIM
        k_w = win_ref[:, lo:lo + SWA_HEAD_DIM]
        v_w = win_ref[:, kvw + lo:kvw + lo + SWA_HEAD_DIM]
        k_new = new[:, lo:lo + SWA_HEAD_DIM]
        v_new = new[:, kvw + lo:kvw + lo + SWA_HEAD_DIM]
        s = lax.dot_general(q.astype(BF16), k_w.astype(BF16), (((1,), (1,)), ((), ())),
                            preferred_element_type=F32) * scale - slopes_ref[g] * dist.astype(F32)
        s = jnp.where(dist <= SWA_WINDOW, s, MASKED)
        s_new = jnp.sum(rounded(q) * rounded(k_new), axis=-1, keepdims=True) * scale
        sink = sink_ref[g]
        m = jnp.maximum(jnp.maximum(jnp.max(s, axis=-1, keepdims=True), s_new), sink)
        e = jnp.exp(s - m)
        e_new = jnp.exp(s_new - m)
        den = jnp.sum(e, axis=-1, keepdims=True) + e_new + jnp.exp(sink - m)
        num = jnp.dot(e.astype(BF16), v_w.astype(BF16), preferred_element_type=F32)
        o_ref[g] = (num + rounded(e_new) * rounded(v_new)) / den
    buf = win_ref[...]
    row = lax.broadcasted_iota(jnp.int32, buf.shape, 0)
    win_out_ref[...] = jnp.where(row == w - 1, new, pltpu.roll(buf, shift=w - 1, axis=0))


def swa_decode(q, new_row, win, slopes, sink):
    db = q.shape[0]
    per_b = lambda shape: pl.BlockSpec((None,) + shape, lambda b: (b,) + (0,) * len(shape))
    full = lambda shape: pl.BlockSpec(shape, lambda b: (0,) * len(shape))
    return pl.pallas_call(
        _swa_decode_kernel,
        out_shape=(jax.ShapeDtypeStruct(q.shape, F32), jax.ShapeDtypeStruct(win.shape, F32)),
        grid=(db,),
        in_specs=[per_b(q.shape[1:]), per_b(new_row.shape[1:]), per_b(win.shape[1:]),
                  full((SWA_KV, SWA_REP, 1)), full((SWA_KV, SWA_REP, 1))],
        out_specs=(per_b(q.shape[1:]), per_b(win.shape[1:])),
        compiler_params=_cparams(("parallel",)),
        name="swa_decode",
    )(q, new_row, win, slopes.reshape(SWA_KV, SWA_REP, 1), sink.reshape(SWA_KV, SWA_REP, 1))


def _even_sample(h, g, w_in, w_out, gate_b, forget_b, pe, phi, slopes,
                 cache_nsa, cache_fox_kv, cache_fox_logf, win_buf, page_table, e):
    db = h.shape[0]
    past = page_table.shape[1] * PAGE_SIZE
    n_pick = NSA_N_SEL - 1
    assert past % NSA_BLOCK == 0 and past // NSA_BLOCK >= n_pick
    n_even, pool = cache_nsa.shape[:2]
    page_table = page_table + e * pool
    cache_nsa = cache_nsa.reshape((n_even * pool,) + cache_nsa.shape[2:])
    cache_fox_kv = cache_fox_kv.reshape((n_even * pool,) + cache_fox_kv.shape[2:])
    cache_fox_logf = cache_fox_logf.reshape((n_even * pool,) + cache_fox_logf.shape[2:])
    win_buf = win_buf.reshape((n_even * db,) + win_buf.shape[2:])[e * db:(e + 1) * db]
    pool = n_even * pool
    z = prenorm_matmul(h, g[0], w_in)
    heads = lambda lo, hi: z[:, lo:hi].reshape(db, -1, HEAD_DIM)
    gz = z[:, E_GATE:E_GATE + LANES].reshape(db, 1, LANES)
    q_a = heads(E_QA, E_KVA)
    cache_n = cache_nsa.reshape(pool, PAGE_SIZE, 4 * NSA_KV * HEAD_DIM)
    pe_lanes = jnp.concatenate([pe[0]] * NSA_KV + [pe[1]] * NSA_KV, axis=1)
    pooled = nsa_pool_pages(cache_n, page_table, pe_lanes)
    o_cmp, idx = nsa_sample_compress_select(q_a, pooled, phi, slopes, gz, gate_b, q_pos=past, n_pick=n_pick)
    grp = lambda x: x.reshape(db, NSA_KV, NSA_REP, HEAD_DIM)
    w = win_buf.shape[1]
    o_a, new_win = nsa_sample_attend(cache_n, idx[..., 0], page_table, grp(q_a), heads(E_KVA, E_QB),
                                     win_buf.reshape(db, w, -1), grp(o_cmp), slopes, gz, gate_b,
                                     n_pick=n_pick, q_pos=past)
    o_b, lf_new = fox_decode(cache_fox_kv.reshape(pool, PAGE_SIZE, -1), cache_fox_logf.transpose(0, 2, 1),
                             page_table, heads(E_QB, E_KB), heads(E_KB, E_VB), heads(E_VB, E_GATE),
                             z[:, E_GATE + E_FORGET_LANE:E_GATE + E_FORGET_LANE + FOX_HEADS].reshape(db, -1, 1),
                             forget_b)
    h = concat_matmul_postnorm_residual([o_a.reshape(db, -1), o_b.reshape(db, -1)], w_out, g[1], h)
    win_off = E_KVA + 4 * NSA_KV * HEAD_DIM
    outs = (z[:, E_KVA:win_off].reshape(db, 1, 4, NSA_KV, HEAD_DIM),
            z[:, E_KB:E_GATE].reshape(db, 1, 2, FOX_HEADS, HEAD_DIM),
            lf_new.reshape(db, 1, FOX_HEADS),
            new_win.reshape(win_buf.shape))
    return h, outs


def _odd_sample(h, g, w_in, w_out, sink, slopes, win_buf):
    db = h.shape[0]
    w = win_buf.shape[1]
    z = prenorm_matmul(h, g[0], w_in)
    q = z[:, :O_K].reshape(db, SWA_KV, SWA_REP, SWA_HEAD_DIM)
    new_row = z[:, O_K:].reshape(db, 2, SWA_KV, 2, SWA_HEAD_DIM)[:, :, :, 0].reshape(db, 1, -1)
    o, new_win = swa_decode(q, new_row, win_buf.reshape(db, w, -1), slopes, sink)
    h = concat_matmul_postnorm_residual([o.reshape(db, -1)], w_out, g[1], h)
    return h, new_win.reshape(win_buf.shape)


def kernel(x_prompt, x_sample, cache_nsa_kv, cache_fox_kv, cache_fox_logf, state_nsa_win, state_swa_win,
           page_table, p_prompt, p_sample, w_in_even, w_out_even, nsa_gate_b, nsa_pe, nsa_phi, fox_forget_b,
           w_in_odd, w_out_odd, swa_sink, norm_g, w_ffn_up, w_ffn_down, w_ple_proj, w_ple_gate):
    b, s, d = x_prompt.shape
    db, dec_seq, _ = x_sample.shape
    assert dec_seq == 1, "the decode kernels handle one new token per sequence"
    depth = norm_g.shape[0]
    nsa_slopes = _alibi_slopes(NSA_HEADS)
    swa_slopes = _alibi_slopes(SWA_HEADS)
    hp = x_prompt.reshape(b * s, d)
    hs = x_sample.reshape(db, d)
    even_p, even_s, odd_p, odd_s = [], [], [], []
    for i in range(depth):
        g = norm_g[i]
        if i % 2 == 0:
            e = i // 2
            w_in = _prep_even_w_in(w_in_even[e])
            w_out = w_out_even[e].astype(BF16)
            gate_b = _lane_row(nsa_gate_b[e], 0)
            hp, outs = _even_prompt(hp, b, s, g, w_in, w_out, gate_b,
                                    _lane_row(fox_forget_b[e], E_FORGET_LANE), nsa_pe[e], nsa_phi[e], nsa_slopes)
            even_p.append(outs)
            hs, outs = _even_sample(hs, g, w_in, w_out, gate_b, fox_forget_b[e], nsa_pe[e], nsa_phi[e],
                                    nsa_slopes, cache_nsa_kv, cache_fox_kv, cache_fox_logf,
                                    state_nsa_win, page_table, e)
            even_s.append(outs)
        else:
            o = i // 2
            w_in = _prep_odd_w_in(w_in_odd[o])
            w_out = w_out_odd[o].astype(BF16)
            hp, tail = _odd_prompt(hp, b, s, g, w_in, w_out, swa_sink[o], swa_slopes)
            odd_p.append(tail)
            hs, new_win = _odd_sample(hs, g, w_in, w_out, swa_sink[o], swa_slopes, state_swa_win[o])
            odd_s.append(new_win)
        ffn_w = (w_ffn_up[i].astype(BF16), w_ffn_down[i].astype(BF16),
                 w_ple_proj[i].astype(BF16), w_ple_gate[i].astype(BF16))
        hp = _ffn_and_ple(hp, p_prompt[i].reshape(b * s, -1), g, *ffn_w)
        hs = _ffn_and_ple(hs, p_sample[i].reshape(db, -1), g, *ffn_w)
    stack = lambda seq, k: jnp.stack([t[k] for t in seq])
    return (hp.reshape(b, s, d), hs.reshape(db, 1, d),
            stack(even_p, 0), stack(even_s, 0), stack(even_p, 1), stack(even_s, 1),
            stack(even_p, 2), stack(even_s, 2), stack(even_p, 3), stack(even_s, 3),
            jnp.stack(odd_p), jnp.stack(odd_s))
```

```python
import functools

import jax
import jax.numpy as jnp
from jax import lax
from jax.experimental import pallas as pl
from jax.experimental.pallas import tpu as pltpu

F32 = jnp.float32
BF16 = jnp.bfloat16

HEAD_DIM = 128
NSA_HEADS = 8
NSA_KV = 2
NSA_REP = NSA_HEADS // NSA_KV
NSA_BLOCK = 64
NSA_BLOCK_SHIFT = 6
NSA_N_SEL = 16
NSA_WINDOW = 512
NSA_FORCE = 1e4
FOX_HEADS = 8
SWA_HEADS = 32
SWA_KV = 4
SWA_REP = SWA_HEADS // SWA_KV
SWA_HEAD_DIM = 64
SWA_WINDOW = 128
PAGE_SIZE = 128
EPS = 1e-6

LANES = 128
MASKED = -1e30
LOG2E = 1.4426950408889634
LOGIT_CHUNK_VREGS = 128
VMEM_LIMIT = 56 * 1024 * 1024

E_QA = 0
E_KVA = NSA_HEADS * HEAD_DIM
E_QB = E_KVA + 6 * NSA_KV * HEAD_DIM
E_KB = E_QB + FOX_HEADS * HEAD_DIM
E_VB = E_KB + FOX_HEADS * HEAD_DIM
E_GATE = E_VB + FOX_HEADS * HEAD_DIM
E_FORGET_LANE = 3 * NSA_HEADS
E_COLS = 6144
O_Q = 0
O_K = SWA_HEADS * SWA_HEAD_DIM
O_V = O_K + SWA_KV * LANES
O_COLS = O_V + SWA_KV * LANES


def _cparams(sem):
    return pltpu.CompilerParams(dimension_semantics=sem, vmem_limit_bytes=VMEM_LIMIT)


def _pick(n, pref):
    t = min(n, pref)
    while n % t:
        t //= 2
    return t


def _rms(x, g):
    return x * lax.rsqrt(jnp.mean(x * x, axis=-1, keepdims=True) + EPS) * g


def _split3(x):
    hi = x.astype(BF16)
    r = x - hi.astype(F32)
    mid = r.astype(BF16)
    lo = (r - mid.astype(F32)).astype(BF16)
    return hi, mid, lo


def _prenorm_mm_kernel(x_ref, g_ref, w_ref, o_ref, xn_ref):
    @pl.when(pl.program_id(1) == 0)
    def _():
        xn_ref[...] = _rms(x_ref[...], g_ref[...]).astype(BF16)

    o_ref[...] = jnp.dot(xn_ref[...], w_ref[...], preferred_element_type=F32)


def prenorm_matmul(x, g, w, *, tm=512, tn=1024):
    m, k = x.shape
    n = w.shape[1]
    tm, tn = _pick(m, tm), _pick(n, tn)
    return pl.pallas_call(
        _prenorm_mm_kernel,
        out_shape=jax.ShapeDtypeStruct((m, n), F32),
        grid=(m // tm, n // tn),
        in_specs=[pl.BlockSpec((tm, k), lambda i, j: (i, 0)),
                  pl.BlockSpec((1, k), lambda i, j: (0, 0)),
                  pl.BlockSpec((k, tn), lambda i, j: (0, j))],
        out_specs=pl.BlockSpec((tm, tn), lambda i, j: (i, j)),
        scratch_shapes=[pltpu.VMEM((tm, k), BF16)],
        compiler_params=_cparams(("parallel", "arbitrary")),
        name="prenorm_matmul",
    )(x, g.reshape(1, k), w)


def _prenorm_swiglu_kernel(x_ref, g_ref, wg_ref, wu_ref, o_ref, xn_ref):
    @pl.when(pl.program_id(1) == 0)
    def _():
        xn_ref[...] = _rms(x_ref[...], g_ref[...]).astype(BF16)

    xn = xn_ref[...]
    gate = jnp.dot(xn, wg_ref[...], preferred_element_type=F32)
    up = jnp.dot(xn, wu_ref[...], preferred_element_type=F32)
    o_ref[...] = (gate * jax.nn.sigmoid(gate) * up).astype(BF16)


def prenorm_swiglu(x, g, w_up, *, tm=512, tn=512):
    m, k = x.shape
    f = w_up.shape[1] // 2
    tm, tn = _pick(m, tm), _pick(f, tn)
    nj = f // tn
    return pl.pallas_call(
        _prenorm_swiglu_kernel,
        out_shape=jax.ShapeDtypeStruct((m, f), BF16),
        grid=(m // tm, nj),
        in_specs=[pl.BlockSpec((tm, k), lambda i, j: (i, 0)),
                  pl.BlockSpec((1, k), lambda i, j: (0, 0)),
                  pl.BlockSpec((k, tn), lambda i, j: (0, j)),
                  pl.BlockSpec((k, tn), lambda i, j: (0, j + nj))],
        out_specs=pl.BlockSpec((tm, tn), lambda i, j: (i, j)),
        scratch_shapes=[pltpu.VMEM((tm, k), BF16)],
        compiler_params=_cparams(("parallel", "arbitrary")),
        name="prenorm_swiglu",
    )(x, g.reshape(1, k), w_up, w_up)


def _mmn_postnorm_res_kernel(*refs, k_sizes):
    n_in = len(k_sizes)
    a_refs, (w_ref, g_ref, h_ref, o_ref) = refs[:n_in], refs[n_in:]
    acc = None
    off = 0
    for a_ref, ks in zip(a_refs, k_sizes):
        part = jnp.dot(a_ref[...].astype(BF16), w_ref[off:off + ks, :], preferred_element_type=F32)
        acc = part if acc is None else acc + part
        off += ks
    o_ref[...] = h_ref[...] + _rms(acc, g_ref[...])


def concat_matmul_postnorm_residual(a_list, w, g, h, *, tm=256):
    m, n = h.shape
    k_sizes = tuple(a.shape[1] for a in a_list)
    tm = _pick(m, tm)
    return pl.pallas_call(
        functools.partial(_mmn_postnorm_res_kernel, k_sizes=k_sizes),
        out_shape=jax.ShapeDtypeStruct((m, n), F32),
        grid=(m // tm,),
        in_specs=[pl.BlockSpec((tm, ks), lambda i: (i, 0)) for ks in k_sizes]
        + [pl.BlockSpec(w.shape, lambda i: (0, 0)),
           pl.BlockSpec((1, n), lambda i: (0, 0)),
           pl.BlockSpec((tm, n), lambda i: (i, 0))],
        out_specs=pl.BlockSpec((tm, n), lambda i: (i, 0)),
        compiler_params=_cparams(("parallel",)),
        name="concat_matmul_postnorm_residual",
    )(*a_list, w, g.reshape(1, n), h)


def _lane_pick(x, lane):
    idx = lax.broadcasted_iota(jnp.int32, x.shape, x.ndim - 1)
    return jnp.sum(jnp.where(idx == lane, x, 0.0), axis=-1, keepdims=True)


def _nsa_cmp_kernel(slopes_ref, q_ref, ksrc_ref, vsrc_ref, pe_ref, phi_ref, gz_ref, gb_ref,
                    o_ref, sel_ref, kc_ref, vc_ref, *, tq, nb, nbp):
    g = pl.program_id(1)
    qi = pl.program_id(2)

    @pl.when(qi == 0)
    def _():
        for src, dst, c in ((ksrc_ref, kc_ref, 0), (vsrc_ref, vc_ref, 1)):
            x = src[...].reshape(nb, NSA_BLOCK, HEAD_DIM) + pe_ref[c][None]
            pooled = jnp.mean(x, axis=1)
            dst[...] = jnp.zeros_like(dst)
            dst[0:nb, :] = jnp.dot(pooled.astype(BF16), phi_ref[c].astype(BF16),
                                   preferred_element_type=F32)

    q = q_ref[...]
    q_pos = qi * tq + lax.broadcasted_iota(jnp.int32, (tq, nbp), 0)
    n_idx = lax.broadcasted_iota(jnp.int32, (tq, nbp), 1)
    complete = (n_idx + 1) * NSA_BLOCK - 1 <= q_pos
    valid = complete & (n_idx < nb)
    dist = q_pos.astype(F32) - (n_idx.astype(F32) * NSA_BLOCK + (NSA_BLOCK - 1) / 2)
    kcb = kc_ref[...].astype(BF16)
    vcb = vc_ref[...].astype(BF16)
    gates = jax.nn.sigmoid(gz_ref[...] + gb_ref[...])
    scale = HEAD_DIM ** -0.5
    imp = jnp.zeros((tq, nbp), F32)
    for r in range(NSA_REP):
        qr = q[:, r * HEAD_DIM:(r + 1) * HEAD_DIM].astype(BF16)
        s = lax.dot_general(qr, kcb, (((1,), (1,)), ((), ())), preferred_element_type=F32) * scale
        s = s - slopes_ref[g * NSA_REP + r] * dist
        s = jnp.where(valid, s, MASKED)
        m = jnp.max(s, axis=-1, keepdims=True)
        m = jnp.where(m > 0.5 * MASKED, m, 0.0)
        e = jnp.exp(s - m)
        p = e / jnp.maximum(jnp.sum(e, axis=-1, keepdims=True), 1e-30)
        imp = imp + p
        o_r = jnp.dot(p.astype(BF16), vcb, preferred_element_type=F32)
        o_ref[:, r * HEAD_DIM:(r + 1) * HEAD_DIM] = _lane_pick(gates, g * NSA_REP + r) * o_r

    cur = jnp.right_shift(q_pos, NSA_BLOCK_SHIFT)
    forced = (n_idx == 0) | (n_idx == cur) | (n_idx == cur - 1)
    score = jnp.where(forced, NSA_FORCE, jnp.where(complete, imp, -NSA_FORCE))
    score = jnp.where(n_idx < nb, score, -jnp.inf)
    rank = jnp.zeros((tq, nbp), F32)
    for i in range(nb):
        ci = score[:, i:i + 1]
        rank = rank + jnp.where((ci > score) | ((ci == score) & (n_idx > i)), 1.0, 0.0)
    sel_ref[...] = jnp.where((rank < min(NSA_N_SEL, nb)) & (n_idx < nb), 1.0, 0.0)


def nsa_compress_select(z, slopes, pe, phi, gate_b, *, tq=256):
    b, s, _ = z.shape
    nb = s // NSA_BLOCK
    assert s % NSA_BLOCK == 0 and nb % 8 == 0 and nb <= LANES
    tq = _pick(s, tq)
    cb = lambda off: off // HEAD_DIM
    kern = functools.partial(_nsa_cmp_kernel, tq=tq, nb=nb, nbp=LANES)
    return pl.pallas_call(
        kern,
        out_shape=(jax.ShapeDtypeStruct((b, s, NSA_HEADS * HEAD_DIM), F32),
                   jax.ShapeDtypeStruct((b, NSA_KV, s, LANES), F32)),
        grid=(b, NSA_KV, s // tq),
        in_specs=[pl.BlockSpec(memory_space=pltpu.SMEM),
                  pl.BlockSpec((None, tq, NSA_REP * HEAD_DIM), lambda bi, g, i: (bi, i, g)),
                  pl.BlockSpec((None, s, HEAD_DIM), lambda bi, g, i: (bi, 0, cb(E_KVA) + g)),
                  pl.BlockSpec((None, s, HEAD_DIM), lambda bi, g, i: (bi, 0, cb(E_KVA) + NSA_KV + g)),
                  pl.BlockSpec(pe.shape, lambda bi, g, i: (0, 0, 0)),
                  pl.BlockSpec(phi.shape, lambda bi, g, i: (0, 0, 0)),
                  pl.BlockSpec((None, tq, LANES), lambda bi, g, i: (bi, i, cb(E_GATE))),
                  pl.BlockSpec((1, LANES), lambda bi, g, i: (0, 0))],
        out_specs=(pl.BlockSpec((None, tq, NSA_REP * HEAD_DIM), lambda bi, g, i: (bi, i, g)),
                   pl.BlockSpec((None, None, tq, LANES), lambda bi, g, i: (bi, g, i, 0))),
        scratch_shapes=[pltpu.VMEM((LANES, HEAD_DIM), F32), pltpu.VMEM((LANES, HEAD_DIM), F32)],
        compiler_params=_cparams(("parallel", "parallel", "arbitrary")),
        name="nsa_compress_select",
    )(slopes, z, z, z, pe, phi, z, gate_b)


def _kv_tile(qi, j, *, tq, tk, window):
    q0 = qi * tq
    first = 0 if window is None else lax.div(jnp.maximum(q0 - window, 0), tk)
    last = lax.div(q0 + tq - 1, tk)
    return first + j, last


def _flash_kernel(*refs, mode, tq, tk, rep, nsteps, window, branch, rc):
    it = iter(refs)
    slopes_ref = next(it) if mode in ("sel", "win", "swa") else None
    sink_ref = next(it) if mode == "swa" else None
    q_ref, k_ref, v_ref = next(it), next(it), next(it)
    sel_ref = next(it) if mode == "sel" else None
    fq_ref, fk_ref = (next(it), next(it)) if mode == "fox" else (None, None)
    prev_ref, gz_ref, gb_ref = (next(it), next(it), next(it)) if mode in ("sel", "win") else (None,) * 3
    o_ref, m_ref, l_ref, acc_ref = next(it), next(it), next(it), next(it)

    g = pl.program_id(1)
    qi = pl.program_id(2)
    j = pl.program_id(3)
    kj, kv_last = _kv_tile(qi, j, tq=tq, tk=tk, window=window)
    head_dim = SWA_HEAD_DIM if mode == "swa" else HEAD_DIM
    qscale = head_dim ** -0.5 * LOG2E
    q0 = qi * tq
    n_chunks = tq // rc
    row_in_chunk = lax.broadcasted_iota(jnp.int32, (rc, 1), 0)

    @pl.when(j == 0)
    def _():
        for r in range(rep):
            if mode == "swa":
                slope = slopes_ref[g * rep + r] * LOG2E
                for c in range(n_chunks):
                    rows = slice(c * rc, (c + 1) * rc)
                    m_ref[r, rows] = sink_ref[g * rep + r] * LOG2E + slope * row_in_chunk.astype(F32)
                l_ref[r] = jnp.ones((tq, 1), F32)
            else:
                m_ref[r] = jnp.full((tq, 1), MASKED, F32)
                l_ref[r] = jnp.zeros((tq, 1), F32)
        acc_ref[...] = jnp.zeros_like(acc_ref)

    @pl.when(kj <= kv_last)
    def _():
        k = k_ref[...].astype(BF16)
        v = v_ref[...].astype(BF16)
        k0 = kj * tk
        col = lax.broadcasted_iota(jnp.int32, (1, tk), 1)
        diff = lax.broadcasted_iota(jnp.int32, (rc, tk), 0) - lax.broadcasted_iota(jnp.int32, (rc, tk), 1)
        if mode == "sel":
            blk_of_key = jnp.right_shift(k0 + lax.broadcasted_iota(jnp.int32, (tk, LANES), 0),
                                         NSA_BLOCK_SHIFT)
            onehot = jnp.where(blk_of_key == lax.broadcasted_iota(jnp.int32, (tk, LANES), 1), 1.0, 0.0)
            k = jnp.concatenate([k, onehot.astype(BF16)], axis=1)
            penalty = jnp.where(sel_ref[...] > 0.5, 0.0, MASKED).astype(BF16)
        if mode == "fox":
            f_keys = fk_ref[...]
        def scores(r, c):
            rows = slice(c * rc, (c + 1) * rc)
            if mode == "swa":
                qp = q_ref[rows, (r // 2) * LANES:(r // 2 + 1) * LANES]
                upper = lax.broadcasted_iota(jnp.int32, qp.shape, 1) >= SWA_HEAD_DIM
                qr = jnp.where(upper if r % 2 else ~upper, qp, 0.0)
            else:
                qr = q_ref[rows, r * HEAD_DIM:(r + 1) * HEAD_DIM]
            qr = (qr * qscale).astype(BF16)
            if mode == "sel":
                qr = jnp.concatenate([qr, penalty[rows]], axis=1)
            return lax.dot_general(qr, k, (((1,), (1,)), ((), ())), preferred_element_type=F32)

        def update(r, c, s):
            rows = slice(c * rc, (c + 1) * rc)
            shift = q0 + c * rc - k0
            if mode == "fox":
                s = s + (fq_ref[:, c * rc:c * rc + 1] - f_keys) * LOG2E
            else:
                s = s + (slopes_ref[g * rep + r] * LOG2E) * (col - shift).astype(F32)
            mask = diff >= -shift
            if window is not None:
                mask = mask & (diff <= window - shift)
            s = jnp.where(mask, s, MASKED)
            m_prev = m_ref[r, rows]
            m_new = jnp.maximum(m_prev, jnp.max(s, axis=-1, keepdims=True))
            alpha = jnp.exp2(m_prev - m_new)
            p = jnp.exp2(s - m_new)
            l_ref[r, rows] = alpha * l_ref[r, rows] + jnp.sum(p, axis=-1, keepdims=True)
            acc_ref[r, rows] = alpha * acc_ref[r, rows] + jnp.dot(p.astype(BF16), v,
                                                                  preferred_element_type=F32)
            m_ref[r, rows] = m_new

        for r in range(rep):
            for c in range(n_chunks):
                update(r, c, scores(r, c))

    @pl.when(j == nsteps - 1)
    def _():
        if mode in ("sel", "win"):
            gates = jax.nn.sigmoid(gz_ref[...] + gb_ref[...])
        for r in range(rep):
            o = acc_ref[r] / l_ref[r]
            blk = slice(r * HEAD_DIM, (r + 1) * HEAD_DIM)
            if mode in ("sel", "win"):
                gate = _lane_pick(gates, branch * NSA_HEADS + g * rep + r)
                o_ref[:, blk] = prev_ref[:, blk] + gate * o
            elif mode == "fox":
                o_ref[...] = o
            elif r % 2:
                o_even = acc_ref[r - 1] / l_ref[r - 1]
                upper = lax.broadcasted_iota(jnp.int32, o.shape, 1) >= SWA_HEAD_DIM
                o_ref[:, (r // 2) * LANES:(r // 2 + 1) * LANES] = jnp.where(upper, o, o_even)


def _flash_call(mode, z, *, tq, tk, rep, n_kv, window, branch, q_col, k_col, v_col, out_cols,
                smem=(), extra=(), extra_specs=()):
    b, s, _ = z.shape
    tq, tk = _pick(s, tq), _pick(s, tk)
    nq = s // tq

    def first_last(qi):
        first = 0 if window is None else max(qi * tq - window, 0) // tk
        return first, (qi * tq + tq - 1) // tk

    nsteps = max(l - f + 1 for f, l in map(first_last, range(nq)))
    qw = rep * HEAD_DIM if mode != "swa" else rep * SWA_HEAD_DIM

    def kv_map(col):
        def index(bi, g, i, j):
            kj, last = _kv_tile(i, j, tq=tq, tk=tk, window=window)
            return bi, jnp.minimum(kj, last), col + g
        return index

    in_specs = [pl.BlockSpec(memory_space=pltpu.SMEM) for _ in smem]
    in_specs += [pl.BlockSpec((None, tq, qw), lambda bi, g, i, j: (bi, i, q_col + g)),
                 pl.BlockSpec((None, tk, LANES), kv_map(k_col)),
                 pl.BlockSpec((None, tk, LANES), kv_map(v_col))]
    in_specs += list(extra_specs(tq, tk, kv_map))
    rc = min(tq, max(8, LOGIT_CHUNK_VREGS * 8 * LANES // tk))
    kern = functools.partial(_flash_kernel, mode=mode, tq=tq, tk=tk, rep=rep, nsteps=nsteps,
                             window=window, branch=branch, rc=rc)
    return pl.pallas_call(
        kern,
        out_shape=jax.ShapeDtypeStruct((b, s, out_cols), F32),
        grid=(b, n_kv, nq, nsteps),
        in_specs=in_specs,
        out_specs=pl.BlockSpec((None, tq, qw), lambda bi, g, i, j: (bi, i, g)),
        scratch_shapes=[pltpu.VMEM((rep, tq, 1), F32), pltpu.VMEM((rep, tq, 1), F32),
                        pltpu.VMEM((rep, tq, LANES), F32)],
        compiler_params=_cparams(("parallel", "parallel", "parallel", "arbitrary")),
        name="flash_" + mode,
    )(*smem, z, z, z, *extra)


def _nsa_gate_specs(tq):
    return [pl.BlockSpec((None, tq, NSA_REP * HEAD_DIM), lambda bi, g, i, j: (bi, i, g)),
            pl.BlockSpec((None, tq, LANES), lambda bi, g, i, j: (bi, i, E_GATE // LANES)),
            pl.BlockSpec((1, LANES), lambda bi, g, i, j: (0, 0))]


def nsa_selected(z, sel, prev, slopes, gate_b, *, t=512):
    def specs(tq, tk, kv_map):
        return [pl.BlockSpec((None, None, tq, LANES), lambda bi, g, i, j: (bi, g, i, 0))] + _nsa_gate_specs(tq)
    kvc = E_KVA // HEAD_DIM
    return _flash_call("sel", z, tq=t, tk=t, rep=NSA_REP, n_kv=NSA_KV, window=None, branch=1,
                       q_col=0, k_col=kvc + 2 * NSA_KV, v_col=kvc + 3 * NSA_KV,
                       out_cols=NSA_HEADS * HEAD_DIM, smem=(slopes,), extra=(sel, prev, z, gate_b),
                       extra_specs=specs)


def nsa_window(z, prev, slopes, gate_b, *, t=512):
    def specs(tq, tk, kv_map):
        return _nsa_gate_specs(tq)
    kvc = E_KVA // HEAD_DIM
    return _flash_call("win", z, tq=t, tk=t, rep=NSA_REP, n_kv=NSA_KV, window=NSA_WINDOW, branch=2,
                       q_col=0, k_col=kvc + 4 * NSA_KV, v_col=kvc + 5 * NSA_KV,
                       out_cols=NSA_HEADS * HEAD_DIM, smem=(slopes,), extra=(prev, z, gate_b),
                       extra_specs=specs)


def fox_attention(z, f_cum_t, *, t=512):
    def specs(tq, tk, kv_map):
        def fk_map(bi, g, i, j):
            return (bi, g, 0, kv_map(0)(bi, g, i, j)[1])
        return [pl.BlockSpec((None, None, 1, tq), lambda bi, g, i, j: (bi, g, 0, i)),
                pl.BlockSpec((None, None, 1, tk), fk_map)]
    return _flash_call("fox", z, tq=t, tk=t, rep=1, n_kv=FOX_HEADS, window=None, branch=0,
                       q_col=E_QB // HEAD_DIM, k_col=E_KB // HEAD_DIM, v_col=E_VB // HEAD_DIM,
                       out_cols=FOX_HEADS * HEAD_DIM, extra=(f_cum_t, f_cum_t), extra_specs=specs)


def swa_attention(z, slopes, sink, *, tq=256, tk=128):
    return _flash_call("swa", z, tq=tq, tk=tk, rep=SWA_REP, n_kv=SWA_KV, window=SWA_WINDOW, branch=0,
                       q_col=0, k_col=O_K // LANES, v_col=O_V // LANES,
                       out_cols=SWA_HEADS * SWA_HEAD_DIM, smem=(slopes, sink),
                       extra_specs=lambda tq, tk, kv_map: [])


def _fox_gates_kernel(gz_ref, fb_ref, logf_ref, f_ref, ft_ref, *, chunk):
    x = gz_ref[...] + fb_ref[...]
    logf = jnp.minimum(x, 0.0) - jnp.log1p(jnp.exp(-jnp.abs(x)))
    logf_ref[...] = logf
    tri = jnp.where(lax.broadcasted_iota(jnp.int32, (chunk, chunk), 0)
                    >= lax.broadcasted_iota(jnp.int32, (chunk, chunk), 1), 1.0, 0.0).astype(BF16)
    carry = jnp.zeros((1, LANES), F32)
    for c in range(logf.shape[0] // chunk):
        rows = slice(c * chunk, (c + 1) * chunk)
        y = carry
        for part in _split3(logf[rows]):
            y = y + jnp.dot(tri, part, preferred_element_type=F32)
        f_ref[rows, :] = y
        carry = y[chunk - 1:chunk, :]
    ft_ref[...] = f_ref[...].T


def fox_gates(z, forget_b_lanes, *, chunk=256):
    b, s, _ = z.shape
    chunk = _pick(s, chunk)
    shp = jax.ShapeDtypeStruct((b, s, LANES), F32)
    return pl.pallas_call(
        functools.partial(_fox_gates_kernel, chunk=chunk),
        out_shape=(shp, shp, jax.ShapeDtypeStruct((b, LANES, s), F32)),
        grid=(b,),
        in_specs=[pl.BlockSpec((None, s, LANES), lambda bi: (bi, 0, E_GATE // LANES)),
                  pl.BlockSpec((1, LANES), lambda bi: (0, 0))],
        out_specs=(pl.BlockSpec((None, s, LANES), lambda bi: (bi, 0, 0)),
                   pl.BlockSpec((None, s, LANES), lambda bi: (bi, 0, 0)),
                   pl.BlockSpec((None, LANES, s), lambda bi: (bi, 0, 0))),
        compiler_params=_cparams(("parallel",)),
        name="fox_gates",
    )(z, forget_b_lanes)


def _alibi_slopes(n):
    return 2.0 ** (-8.0 * jnp.arange(1, n + 1, dtype=F32) / n)


def _prep_even_w_in(w):
    d = w.shape[0]
    n_gate = 3 * NSA_HEADS
    src_gate = E_QB
    src_qb = src_gate + n_gate
    src_forget = src_qb + 3 * FOX_HEADS * HEAD_DIM
    used = E_GATE + n_gate + FOX_HEADS
    return jnp.concatenate([w[:, :src_gate], w[:, src_qb:src_forget], w[:, src_gate:src_qb],
                            w[:, src_forget:], jnp.zeros((d, E_COLS - used), w.dtype)], axis=1).astype(BF16)


def _prep_odd_w_in(w):
    d = w.shape[0]
    kw = SWA_KV * SWA_HEAD_DIM
    k = w[:, O_K:O_K + kw].reshape(d, SWA_KV, SWA_HEAD_DIM)
    v = w[:, O_K + kw:].reshape(d, SWA_KV, SWA_HEAD_DIM)
    dup = lambda x: jnp.concatenate([x, x], axis=-1).reshape(d, SWA_KV * LANES)
    return jnp.concatenate([w[:, :O_K], dup(k), dup(v)], axis=1).astype(BF16)


def _lane_row(values, offset):
    return jnp.zeros((1, LANES), F32).at[0, offset:offset + values.shape[0]].set(values)


def _ffn_down_ple_kernel(a_ref, wd_ref, g3_ref, h_ref, p_ref, wpp_ref, wpg_ref, g4_ref, o_ref):
    y = jnp.dot(a_ref[...], wd_ref[...], preferred_element_type=F32)
    h = h_ref[...] + _rms(y, g3_ref[...])
    proj = jnp.dot(p_ref[...].astype(BF16), wpp_ref[...], preferred_element_type=F32)
    gate = jnp.dot(h.astype(BF16), wpg_ref[...], preferred_element_type=F32)
    o_ref[...] = h + _rms(proj, g4_ref[...]) * jax.nn.sigmoid(gate)


def ffn_down_ple(act, w_down, g3, h, p, w_pp, w_pg, g4, *, tm=256):
    m, d = h.shape
    f = act.shape[1]
    pd = p.shape[1]
    tm = _pick(m, tm)
    row = lambda width: pl.BlockSpec((tm, width), lambda i: (i, 0))
    whole = lambda shape: pl.BlockSpec(shape, lambda i: (0, 0))
    return pl.pallas_call(
        _ffn_down_ple_kernel,
        out_shape=jax.ShapeDtypeStruct((m, d), F32),
        grid=(m // tm,),
        in_specs=[row(f), whole((f, d)), whole((1, d)), row(d), row(pd), whole((pd, d)), whole((d, d)),
                  whole((1, d))],
        out_specs=row(d),
        compiler_params=_cparams(("parallel",)),
        name="ffn_down_ple",
    )(act, w_down, g3.reshape(1, d), h, p, w_pp, w_pg, g4.reshape(1, d))


def _ffn_and_ple(h, p, g, w_up, w_down, w_pp, w_pg):
    act = prenorm_swiglu(h, g[2], w_up)
    return ffn_down_ple(act, w_down, g[3], h, p, w_pp, w_pg, g[4])


def _even_prompt(h, b, s, g, w_in, w_out, gate_b, forget_b, pe, phi, slopes):
    z = prenorm_matmul(h, g[0], w_in).reshape(b, s, E_COLS)
    o_a, sel = nsa_compress_select(z, slopes, pe, phi, gate_b)
    o_a = nsa_selected(z, sel, o_a, slopes, gate_b)
    o_a = nsa_window(z, o_a, slopes, gate_b)
    logf, f_cum, f_cum_t = fox_gates(z, forget_b)
    fl = slice(E_FORGET_LANE, E_FORGET_LANE + FOX_HEADS)
    o_b = fox_attention(z, f_cum_t[:, fl, :].reshape(b, FOX_HEADS, 1, s))
    m = b * s
    h = concat_matmul_postnorm_residual([o_a.reshape(m, -1), o_b.reshape(m, -1)], w_out, g[1], h)
    wb = min(NSA_WINDOW, s)
    win_off = E_KVA + 4 * NSA_KV * HEAD_DIM
    outs = (z[:, :, E_KVA:win_off].reshape(b, s, 4, NSA_KV, HEAD_DIM),
            z[:, :, E_KB:E_GATE].reshape(b, s, 2, FOX_HEADS, HEAD_DIM),
            logf[:, :, fl],
            z[:, s - wb:, win_off:E_QB].reshape(b, wb, 2, NSA_KV, HEAD_DIM))
    return h, outs


def _odd_prompt(h, b, s, g, w_in, w_out, sink, slopes):
    z = prenorm_matmul(h, g[0], w_in).reshape(b, s, O_COLS)
    o = swa_attention(z, slopes, sink)
    h = concat_matmul_postnorm_residual([o.reshape(b * s, -1)], w_out, g[1], h)
    wb = min(SWA_WINDOW, s)
    tail = z[:, s - wb:, O_K:].reshape(b, wb, 2, SWA_KV, 2, SWA_HEAD_DIM)[:, :, :, :, 0]
    return h, tail


NSA_POOL_PAGES = 16
FOX_PAGES = 8


def _nsa_pool_kernel(pt_ref, *refs, n_pages):
    page_refs, (pe_ref, o_ref) = refs[:n_pages], refs[n_pages:]
    blocks_per_page = PAGE_SIZE // NSA_BLOCK
    for i, ref in enumerate(page_refs):
        x = ref[...].reshape((blocks_per_page, NSA_BLOCK) + ref.shape[1:]) + pe_ref[...][None]
        o_ref[i * blocks_per_page:(i + 1) * blocks_per_page] = jnp.mean(x, axis=1)


def nsa_pool_pages(cache, page_table, pe_slabs):
    db, n_pages = page_table.shape
    slabs = cache.shape[2:]
    p = _pick(n_pages, NSA_POOL_PAGES)
    bpp = PAGE_SIZE // NSA_BLOCK

    def page_map(i):
        return lambda b, c, pt: (pt[b, c * p + i], 0, 0, 0)

    return pl.pallas_call(
        functools.partial(_nsa_pool_kernel, n_pages=p),
        out_shape=jax.ShapeDtypeStruct((db, n_pages * bpp) + slabs, F32),
        grid_spec=pltpu.PrefetchScalarGridSpec(
            num_scalar_prefetch=1,
            grid=(db, n_pages // p),
            in_specs=[pl.BlockSpec((PAGE_SIZE, None) + slabs, page_map(i)) for i in range(p)]
            + [pl.BlockSpec((NSA_BLOCK,) + slabs, lambda b, c, pt: (0, 0, 0))],
            out_specs=pl.BlockSpec((None, p * bpp) + slabs, lambda b, c, pt: (b, c, 0, 0))),
        compiler_params=_cparams(("parallel", "arbitrary")),
        name="nsa_pool_pages",
    )(page_table, *([cache] * p), pe_slabs)


def _transposed_rows(row):
    n = row.shape[1]
    bc = jnp.broadcast_to(row, (n, n))
    return bc.T, bc


def _nsa_sample_cmp_kernel(q_ref, pooled_ref, phi_ref, slopes_ref, gz_ref, gb_ref, o_ref, idx_ref,
                           *, q_pos, n_pick):
    n_slabs = 2 * NSA_KV
    npb = pooled_ref.shape[0] // n_slabs
    q = q_ref[...].astype(BF16)
    slab = lambda s: pooled_ref[pl.ds(s, npb, stride=n_slabs), :].astype(BF16)
    n_row = lax.broadcasted_iota(jnp.int32, (1, npb), 1)
    complete = (n_row + 1) * NSA_BLOCK - 1 <= q_pos
    dist = q_pos - (n_row.astype(F32) * NSA_BLOCK + (NSA_BLOCK - 1) / 2)
    head_row = lax.broadcasted_iota(jnp.int32, (NSA_HEADS, 1), 0)
    gates = jax.nn.sigmoid(gz_ref[...] + gb_ref[...])
    lane = lax.broadcasted_iota(jnp.int32, (NSA_HEADS, LANES), 1)
    gate_cmp = jnp.sum(jnp.where(lane == head_row, gates, 0.0), axis=-1, keepdims=True)
    scale = HEAD_DIM ** -0.5
    i_idx = lax.broadcasted_iota(jnp.int32, (npb, npb), 0)
    j_idx = lax.broadcasted_iota(jnp.int32, (npb, npb), 1)
    o_acc = jnp.zeros((NSA_HEADS, HEAD_DIM), F32)
    for g in range(NSA_KV):
        in_group = (head_row >= g * NSA_REP) & (head_row < (g + 1) * NSA_REP)
        kc = jnp.dot(slab(g), phi_ref[0].astype(BF16), preferred_element_type=F32).astype(BF16)
        vc = jnp.dot(slab(NSA_KV + g), phi_ref[1].astype(BF16), preferred_element_type=F32).astype(BF16)
        s = lax.dot_general(q, kc, (((1,), (1,)), ((), ())), preferred_element_type=F32) * scale
        s = s - slopes_ref[...] * dist
        s = jnp.where(complete, s, MASKED)
        m = jnp.max(s, axis=-1, keepdims=True)
        m = jnp.where(m > 0.5 * MASKED, m, 0.0)
        e = jnp.exp(s - m)
        p = e / jnp.maximum(jnp.sum(e, axis=-1, keepdims=True), 1e-30)
        o_g = jnp.dot(p.astype(BF16), vc, preferred_element_type=F32)
        o_acc = jnp.where(in_group, o_g, o_acc)
        imp = jnp.sum(jnp.where(in_group, p, 0.0), axis=0, keepdims=True)

        cur = q_pos // NSA_BLOCK
        forced = (n_row == 0) | (n_row == cur) | (n_row == cur - 1)
        score = jnp.where(forced, NSA_FORCE, jnp.where(complete, imp, -NSA_FORCE))
        col, bc = _transposed_rows(score)
        rank = jnp.sum(jnp.where((col > bc) | ((col == bc) & (i_idx < j_idx)), 1.0, 0.0),
                       axis=0, keepdims=True)
        sel = jnp.where(rank < n_pick, 1.0, 0.0)
        sel_col, _ = _transposed_rows(sel)
        pos = jnp.sum(jnp.where(i_idx < j_idx, sel_col, 0.0), axis=0, keepdims=True)
        slot = lax.broadcasted_iota(jnp.int32, (NSA_N_SEL, npb), 0).astype(F32)
        n_f = lax.broadcasted_iota(jnp.int32, (NSA_N_SEL, npb), 1).astype(F32)
        hit = (sel > 0.5) & (pos == slot)
        ids = jnp.sum(jnp.where(hit, n_f, 0.0), axis=-1, keepdims=True)
        idx_ref[g] = jnp.broadcast_to(ids, (NSA_N_SEL, LANES)).astype(jnp.int32)
    o_ref[...] = gate_cmp * o_acc


def nsa_sample_compress_select(q, pooled, phi, slopes, gz, gate_b, *, q_pos, n_pick):
    db = q.shape[0]
    pooled = pooled.reshape(db, -1, HEAD_DIM)
    npb = pooled.shape[1]
    kern = functools.partial(_nsa_sample_cmp_kernel, q_pos=q_pos, n_pick=n_pick)
    return pl.pallas_call(
        kern,
        out_shape=(jax.ShapeDtypeStruct((db, NSA_HEADS, HEAD_DIM), F32),
                   jax.ShapeDtypeStruct((db, NSA_KV, NSA_N_SEL, LANES), jnp.int32)),
        grid=(db,),
        in_specs=[pl.BlockSpec((None, NSA_HEADS, HEAD_DIM), lambda b: (b, 0, 0)),
                  pl.BlockSpec((None, npb, pooled.shape[2]), lambda b: (b, 0, 0)),
                  pl.BlockSpec(phi.shape, lambda b: (0, 0, 0)),
                  pl.BlockSpec((NSA_HEADS, 1), lambda b: (0, 0)),
                  pl.BlockSpec((None, 1, LANES), lambda b: (b, 0, 0)),
                  pl.BlockSpec((1, LANES), lambda b: (0, 0))],
        out_specs=(pl.BlockSpec((None, NSA_HEADS, HEAD_DIM), lambda b: (b, 0, 0)),
                   pl.BlockSpec((None, NSA_KV, NSA_N_SEL, LANES), lambda b: (b, 0, 0, 0))),
        compiler_params=_cparams(("parallel",)),
        name="nsa_sample_compress_select",
    )(q, pooled, phi, slopes.reshape(NSA_HEADS, 1), gz, gate_b)


def _nsa_sample_attend_kernel(idx_ref, pt_ref, *refs, n_pick, q_pos):
    n_blk = NSA_KV * n_pick
    n_slabs = 4 * NSA_KV
    blk_refs = refs[:n_blk]
    q_ref, new_ref, win_ref, prev_ref, slopes_ref, gz_ref, gb_ref, o_ref, win_out_ref = refs[n_blk:]
    b = pl.program_id(0)
    w = win_ref.shape[0]
    scale = HEAD_DIM ** -0.5
    gates = jax.nn.sigmoid(gz_ref[...] + gb_ref[...])
    r_col = lax.broadcasted_iota(jnp.int32, (NSA_REP, 1), 0)
    lane = lax.broadcasted_iota(jnp.int32, (NSA_REP, LANES), 1)
    row_w = lax.broadcasted_iota(jnp.int32, (w, HEAD_DIM), 0)
    n_keys = n_pick * NSA_BLOCK
    key_lane = lax.broadcasted_iota(jnp.int32, (1, n_keys), 1)

    def rounded(x):
        return x.astype(BF16).astype(F32)

    def attend(q4, keys, vals, k_new, v_new, dist, valid, slope):
        s = lax.dot_general(q4.astype(BF16), keys.astype(BF16), (((1,), (1,)), ((), ())),
                            preferred_element_type=F32) * scale - slope * dist
        s = jnp.where(valid, s, MASKED)
        s_new = jnp.sum(rounded(q4) * rounded(k_new), axis=-1, keepdims=True) * scale
        m = jnp.maximum(jnp.max(s, axis=-1, keepdims=True), s_new)
        e = jnp.exp(s - m)
        e_new = jnp.exp(s_new - m)
        den = jnp.sum(e, axis=-1, keepdims=True) + e_new
        num = jnp.dot(e.astype(BF16), vals.astype(BF16), preferred_element_type=F32)
        return (num + rounded(e_new) * rounded(v_new)) / den

    for g in range(NSA_KV):
        q4 = q_ref[g]
        slope = slopes_ref[g]
        new = lambda c: new_ref[c * NSA_KV + g:c * NSA_KV + g + 1, :]
        gate = lambda c: jnp.sum(jnp.where(lane == c * NSA_HEADS + g * NSA_REP + r_col, gates, 0.0),
                                 axis=-1, keepdims=True)
        blocks = blk_refs[g * n_pick:(g + 1) * n_pick]
        slab = lambda r, c: r[pl.ds(c * NSA_KV + g, NSA_BLOCK, stride=n_slabs), :]
        keys = jnp.concatenate([slab(r, 2) for r in blocks], axis=0)
        vals = jnp.concatenate([slab(r, 3) for r in blocks], axis=0)
        k_pos = key_lane & (NSA_BLOCK - 1)
        for i in range(n_pick):
            in_blk = jnp.right_shift(key_lane, NSA_BLOCK_SHIFT) == i
            k_pos = k_pos + jnp.where(in_blk, idx_ref[b, g, i] * NSA_BLOCK, 0)
        dist = q_pos - k_pos
        o_sel = attend(q4, keys, vals, new(2), new(3), dist.astype(F32), dist >= 0, slope)
        k_w = win_ref[:, g * HEAD_DIM:(g + 1) * HEAD_DIM]
        v_w = win_ref[:, (NSA_KV + g) * HEAD_DIM:(NSA_KV + g + 1) * HEAD_DIM]
        dist_w = w - lax.broadcasted_iota(jnp.int32, (1, w), 1)
        o_win = attend(q4, k_w, v_w, new(4), new(5), dist_w.astype(F32), dist_w <= NSA_WINDOW, slope)
        o_ref[g] = prev_ref[g] + gate(1) * o_sel + gate(2) * o_win
        for c, buf in ((0, k_w), (1, v_w)):
            rolled = pltpu.roll(buf, shift=w - 1, axis=0)
            col = (c * NSA_KV + g) * HEAD_DIM
            win_out_ref[:, col:col + HEAD_DIM] = jnp.where(row_w == w - 1, new(4 + c), rolled)


def nsa_sample_attend(cache, idx, page_table, q, new_rows, win, prev, slopes, gz, gate_b, *, n_pick, q_pos):
    db = q.shape[0]
    w = win.shape[1]
    bpp = PAGE_SIZE // NSA_BLOCK
    bpp_shift = bpp.bit_length() - 1

    def blk_map(g, i):
        def index(b, idx_r, pt_r):
            blk = idx_r[b, g, i]
            return pt_r[b, jnp.right_shift(blk, bpp_shift)] * bpp + (blk & (bpp - 1)), 0
        return index

    blk_specs = [pl.BlockSpec((NSA_BLOCK * 4 * NSA_KV, HEAD_DIM), blk_map(g, i))
                 for g in range(NSA_KV) for i in range(n_pick)]
    full = lambda shape: pl.BlockSpec(shape, lambda b, idx_r, pt_r: (0,) * len(shape))
    per_b = lambda shape: pl.BlockSpec((None,) + shape, lambda b, idx_r, pt_r: (b,) + (0,) * len(shape))
    kern = functools.partial(_nsa_sample_attend_kernel, n_pick=n_pick, q_pos=q_pos)
    return pl.pallas_call(
        kern,
        out_shape=(jax.ShapeDtypeStruct((db, NSA_KV, NSA_REP, HEAD_DIM), F32),
                   jax.ShapeDtypeStruct(win.shape, F32)),
        grid_spec=pltpu.PrefetchScalarGridSpec(
            num_scalar_prefetch=2,
            grid=(db,),
            in_specs=blk_specs + [per_b((NSA_KV, NSA_REP, HEAD_DIM)), per_b(new_rows.shape[1:]),
                                  per_b(win.shape[1:]), per_b((NSA_KV, NSA_REP, HEAD_DIM)),
                                  full((NSA_KV, NSA_REP, 1)), per_b((1, LANES)), full((1, LANES))],
            out_specs=(per_b((NSA_KV, NSA_REP, HEAD_DIM)), per_b(win.shape[1:]))),
        compiler_params=_cparams(("arbitrary",)),
        name="nsa_sample_attend",
    )(idx, page_table, *([cache] * len(blk_specs)), q, new_rows, win, prev,
      slopes.reshape(NSA_KV, NSA_REP, 1), gz, gate_b)


def _fox_decode_kernel(pt_ref, *refs, n_pages):
    kv_refs, lf_refs = refs[:n_pages], refs[n_pages:2 * n_pages]
    (q_ref, knew_ref, vnew_ref, vrow_ref, fraw_ref, fb_ref,
     o_ref, lfnew_ref, qbd_ref, m_ref, l_ref, acc_ref, carry_ref) = refs[2 * n_pages:]
    c = pl.program_id(1)
    width = FOX_HEADS * HEAD_DIM
    scale = HEAD_DIM ** -0.5
    head_row = lax.broadcasted_iota(jnp.int32, (FOX_HEADS, width), 0)
    head_of_lane = jnp.right_shift(lax.broadcasted_iota(jnp.int32, (FOX_HEADS, width), 1),
                                   HEAD_DIM.bit_length() - 1)

    @pl.when(c == 0)
    def _():
        q = q_ref[...]
        qbd_ref[...] = jnp.where(head_of_lane == head_row, jnp.concatenate([q] * FOX_HEADS, axis=1),
                                 0.0).astype(BF16)
        x = fraw_ref[...] + fb_ref[...]
        lf_new = jnp.minimum(x, 0.0) - jnp.log1p(jnp.exp(-jnp.abs(x)))
        lfnew_ref[...] = lf_new
        carry_ref[...] = lf_new
        rounded = lambda t: t.astype(BF16).astype(F32)
        m_ref[...] = jnp.sum(rounded(q) * rounded(knew_ref[...]), axis=-1, keepdims=True) * scale
        l_ref[...] = jnp.ones_like(l_ref)
        acc_ref[...] = jnp.broadcast_to(rounded(vrow_ref[...]), acc_ref.shape)

    later = jnp.where(lax.broadcasted_iota(jnp.int32, (PAGE_SIZE, PAGE_SIZE), 0)
                      > lax.broadcasted_iota(jnp.int32, (PAGE_SIZE, PAGE_SIZE), 1), 1.0, 0.0).astype(BF16)
    lf_all = jnp.concatenate([lf_ref[...] for lf_ref in lf_refs], axis=0)
    rows = lf_all.shape[0]
    suffix3 = jnp.dot(jnp.concatenate(_split3(lf_all), axis=0), later, preferred_element_type=F32)
    suffix = suffix3[:rows] + suffix3[rows:2 * rows] + suffix3[2 * rows:]
    totals = jnp.sum(lf_all, axis=-1, keepdims=True)
    carry = carry_ref[...]
    biases = []
    for i in range(n_pages):
        page_rows = slice(i * FOX_HEADS, (i + 1) * FOX_HEADS)
        biases.append(carry + suffix[page_rows])
        carry = carry + totals[page_rows]
    carry_ref[...] = carry
    bias = jnp.concatenate(biases, axis=1)
    def heads_on_lanes(kv_ref, first):
        return jnp.concatenate([kv_ref[pl.ds(first + h, PAGE_SIZE, stride=2 * FOX_HEADS), :].astype(BF16)
                                for h in range(FOX_HEADS)], axis=1)

    kp = jnp.concatenate([heads_on_lanes(kv_ref, 0) for kv_ref in kv_refs], axis=0)
    vp = jnp.concatenate([heads_on_lanes(kv_ref, FOX_HEADS) for kv_ref in kv_refs], axis=0)
    s = lax.dot_general(qbd_ref[...], kp, (((1,), (1,)), ((), ())),
                        preferred_element_type=F32) * scale + bias
    m_prev = m_ref[...]
    m_new = jnp.maximum(m_prev, jnp.max(s, axis=-1, keepdims=True))
    alpha = jnp.exp(m_prev - m_new)
    p = jnp.exp(s - m_new)
    l_ref[...] = alpha * l_ref[...] + jnp.sum(p, axis=-1, keepdims=True)
    acc_ref[...] = alpha * acc_ref[...] + jnp.dot(p.astype(BF16), vp, preferred_element_type=F32)
    m_ref[...] = m_new

    @pl.when(c == pl.num_programs(1) - 1)
    def _():
        acc = jnp.where(head_of_lane == head_row, acc_ref[...], 0.0)
        o = acc[:, :HEAD_DIM]
        for h in range(1, FOX_HEADS):
            o = o + acc[:, h * HEAD_DIM:(h + 1) * HEAD_DIM]
        o_ref[...] = o / l_ref[...]


def fox_decode(cache_kv, cache_logf_t, page_table, q, k_new, v_new, f_raw, forget_b):
    db, n_pages = page_table.shape
    p = _pick(n_pages, FOX_PAGES)
    width = FOX_HEADS * HEAD_DIM
    page_rows = PAGE_SIZE * 2 * FOX_HEADS

    def page_map(i):
        return lambda b, c, pt: (pt[b, n_pages - 1 - (c * p + i)], 0, 0)

    def kv_page_map(i):
        return lambda b, c, pt: (pt[b, n_pages - 1 - (c * p + i)], 0)

    per_b = lambda shape: pl.BlockSpec((None,) + shape, lambda b, c, pt: (b,) + (0,) * len(shape))
    return pl.pallas_call(
        functools.partial(_fox_decode_kernel, n_pages=p),
        out_shape=(jax.ShapeDtypeStruct((db, FOX_HEADS, HEAD_DIM), F32),
                   jax.ShapeDtypeStruct((db, FOX_HEADS, 1), F32)),
        grid_spec=pltpu.PrefetchScalarGridSpec(
            num_scalar_prefetch=1,
            grid=(db, n_pages // p),
            in_specs=[pl.BlockSpec((page_rows, HEAD_DIM), kv_page_map(i)) for i in range(p)]
            + [pl.BlockSpec((None, FOX_HEADS, PAGE_SIZE), page_map(i)) for i in range(p)]
            + [per_b((FOX_HEADS, HEAD_DIM)), per_b((FOX_HEADS, HEAD_DIM)), per_b((FOX_HEADS, HEAD_DIM)),
               per_b((1, width)), per_b((FOX_HEADS, 1)),
               pl.BlockSpec((FOX_HEADS, 1), lambda b, c, pt: (0, 0))],
            out_specs=(per_b((FOX_HEADS, HEAD_DIM)), per_b((FOX_HEADS, 1))),
            scratch_shapes=[pltpu.VMEM((FOX_HEADS, width), BF16), pltpu.VMEM((FOX_HEADS, 1), F32),
                            pltpu.VMEM((FOX_HEADS, 1), F32), pltpu.VMEM((FOX_HEADS, width), F32),
                            pltpu.VMEM((FOX_HEADS, 1), F32)]),
        compiler_params=_cparams(("parallel", "arbitrary")),
        name="fox_decode",
    )(page_table, *([cache_kv] * p), *([cache_logf_t] * p), q, k_new, v_new,
      v_new.reshape(db, 1, width), f_raw, forget_b.reshape(FOX_HEADS, 1))


def _swa_decode_kernel(q_ref, new_ref, win_ref, slopes_ref, sink_ref, o_ref, win_out_ref):
    w = win_ref.shape[0]
    kvw = SWA_KV * SWA_HEAD_DIM
    scale = SWA_HEAD_DIM ** -0.5
    rounded = lambda t: t.astype(BF16).astype(F32)
    dist = (w - lax.broadcasted_iota(jnp.int32, (1, w), 1))
    new = new_ref[...]
    for g in range(SWA_KV):
        q = q_ref[g]
        lo = g * SWA_HEAD_DIM
        k_w = win_ref[:, lo:lo + SWA_HEAD_DIM]
        v_w = win_ref[:, kvw + lo:kvw + lo + SWA_HEAD_DIM]
        k_new = new[:, lo:lo + SWA_HEAD_DIM]
        v_new = new[:, kvw + lo:kvw + lo + SWA_HEAD_DIM]
        s = lax.dot_general(q.astype(BF16), k_w.astype(BF16), (((1,), (1,)), ((), ())),
                            preferred_element_type=F32) * scale - slopes_ref[g] * dist.astype(F32)
        s = jnp.where(dist <= SWA_WINDOW, s, MASKED)
        s_new = jnp.sum(rounded(q) * rounded(k_new), axis=-1, keepdims=True) * scale
        sink = sink_ref[g]
        m = jnp.maximum(jnp.maximum(jnp.max(s, axis=-1, keepdims=True), s_new), sink)
        e = jnp.exp(s - m)
        e_new = jnp.exp(s_new - m)
        den = jnp.sum(e, axis=-1, keepdims=True) + e_new + jnp.exp(sink - m)
        num = jnp.dot(e.astype(BF16), v_w.astype(BF16), preferred_element_type=F32)
        o_ref[g] = (num + rounded(e_new) * rounded(v_new)) / den
    buf = win_ref[...]
    row = lax.broadcasted_iota(jnp.int32, buf.shape, 0)
    win_out_ref[...] = jnp.where(row == w - 1, new, pltpu.roll(buf, shift=w - 1, axis=0))


def swa_decode(q, new_row, win, slopes, sink):
    db = q.shape[0]
    per_b = lambda shape: pl.BlockSpec((None,) + shape, lambda b: (b,) + (0,) * len(shape))
    full = lambda shape: pl.BlockSpec(shape, lambda b: (0,) * len(shape))
    return pl.pallas_call(
        _swa_decode_kernel,
        out_shape=(jax.ShapeDtypeStruct(q.shape, F32), jax.ShapeDtypeStruct(win.shape, F32)),
        grid=(db,),
        in_specs=[per_b(q.shape[1:]), per_b(new_row.shape[1:]), per_b(win.shape[1:]),
                  full((SWA_KV, SWA_REP, 1)), full((SWA_KV, SWA_REP, 1))],
        out_specs=(per_b(q.shape[1:]), per_b(win.shape[1:])),
        compiler_params=_cparams(("parallel",)),
        name="swa_decode",
    )(q, new_row, win, slopes.reshape(SWA_KV, SWA_REP, 1), sink.reshape(SWA_KV, SWA_REP, 1))


def _even_sample(h, g, w_in, w_out, gate_b, forget_b, pe, phi, slopes,
                 cache_nsa, cache_fox_kv, cache_fox_logf, win_buf, page_table, e):
    db = h.shape[0]
    past = page_table.shape[1] * PAGE_SIZE
    n_pick = NSA_N_SEL - 1
    assert past % NSA_BLOCK == 0 and past // NSA_BLOCK >= n_pick
    n_even, pool = cache_nsa.shape[:2]
    page_table = page_table + e * pool
    nsa_rows = cache_nsa.reshape(-1, HEAD_DIM)
    nsa_halves = cache_nsa.reshape(-1, 2, 2 * NSA_KV, HEAD_DIM)
    fox_rows = cache_fox_kv.reshape(-1, HEAD_DIM)
    logf_t = cache_fox_logf.reshape((n_even * pool,) + cache_fox_logf.shape[2:]).transpose(0, 2, 1)
    win_buf = win_buf.reshape((n_even * db,) + win_buf.shape[2:])[e * db:(e + 1) * db]
    z = prenorm_matmul(h, g[0], w_in)
    heads = lambda lo, hi: z[:, lo:hi].reshape(db, -1, HEAD_DIM)
    gz = z[:, E_GATE:E_GATE + LANES].reshape(db, 1, LANES)
    q_a = heads(E_QA, E_KVA)
    pe_slabs = jnp.stack([pe[0]] * NSA_KV + [pe[1]] * NSA_KV, axis=1)
    pooled = nsa_pool_pages(nsa_halves, page_table, pe_slabs)
    o_cmp, idx = nsa_sample_compress_select(q_a, pooled, phi, slopes, gz, gate_b, q_pos=past, n_pick=n_pick)
    grp = lambda x: x.reshape(db, NSA_KV, NSA_REP, HEAD_DIM)
    w = win_buf.shape[1]
    o_a, new_win = nsa_sample_attend(nsa_rows, idx[..., 0], page_table, grp(q_a), heads(E_KVA, E_QB),
                                     win_buf.reshape(db, w, -1), grp(o_cmp), slopes, gz, gate_b,
                                     n_pick=n_pick, q_pos=past)
    o_b, lf_new = fox_decode(fox_rows, logf_t,
                             page_table, heads(E_QB, E_KB), heads(E_KB, E_VB), heads(E_VB, E_GATE),
                             z[:, E_GATE + E_FORGET_LANE:E_GATE + E_FORGET_LANE + FOX_HEADS].reshape(db, -1, 1),
                             forget_b)
    h = concat_matmul_postnorm_residual([o_a.reshape(db, -1), o_b.reshape(db, -1)], w_out, g[1], h)
    win_off = E_KVA + 4 * NSA_KV * HEAD_DIM
    outs = (z[:, E_KVA:win_off].reshape(db, 1, 4, NSA_KV, HEAD_DIM),
            z[:, E_KB:E_GATE].reshape(db, 1, 2, FOX_HEADS, HEAD_DIM),
            lf_new.reshape(db, 1, FOX_HEADS),
            new_win.reshape(win_buf.shape))
    return h, outs


def _odd_sample(h, g, w_in, w_out, sink, slopes, win_buf):
    db = h.shape[0]
    w = win_buf.shape[1]
    z = prenorm_matmul(h, g[0], w_in)
    q = z[:, :O_K].reshape(db, SWA_KV, SWA_REP, SWA_HEAD_DIM)
    new_row = z[:, O_K:].reshape(db, 2, SWA_KV, 2, SWA_HEAD_DIM)[:, :, :, 0].reshape(db, 1, -1)
    o, new_win = swa_decode(q, new_row, win_buf.reshape(db, w, -1), slopes, sink)
    h = concat_matmul_postnorm_residual([o.reshape(db, -1)], w_out, g[1], h)
    return h, new_win.reshape(win_buf.shape)


def kernel(x_prompt, x_sample, cache_nsa_kv, cache_fox_kv, cache_fox_logf, state_nsa_win, state_swa_win,
           page_table, p_prompt, p_sample, w_in_even, w_out_even, nsa_gate_b, nsa_pe, nsa_phi, fox_forget_b,
           w_in_odd, w_out_odd, swa_sink, norm_g, w_ffn_up, w_ffn_down, w_ple_proj, w_ple_gate):
    b, s, d = x_prompt.shape
    db, dec_seq, _ = x_sample.shape
    assert dec_seq == 1, "the decode kernels handle one new token per sequence"
    depth = norm_g.shape[0]
    nsa_slopes = _alibi_slopes(NSA_HEADS)
    swa_slopes = _alibi_slopes(SWA_HEADS)
    hp = x_prompt.reshape(b * s, d)
    hs = x_sample.reshape(db, d)
    even_p, even_s, odd_p, odd_s = [], [], [], []
    for i in range(depth):
        g = norm_g[i]
        if i % 2 == 0:
            e = i // 2
            w_in = _prep_even_w_in(w_in_even[e])
            w_out = w_out_even[e].astype(BF16)
            gate_b = _lane_row(nsa_gate_b[e], 0)
            hp, outs = _even_prompt(hp, b, s, g, w_in, w_out, gate_b,
                                    _lane_row(fox_forget_b[e], E_FORGET_LANE), nsa_pe[e], nsa_phi[e], nsa_slopes)
            even_p.append(outs)
            hs, outs = _even_sample(hs, g, w_in, w_out, gate_b, fox_forget_b[e], nsa_pe[e], nsa_phi[e],
                                    nsa_slopes, cache_nsa_kv, cache_fox_kv, cache_fox_logf,
                                    state_nsa_win, page_table, e)
            even_s.append(outs)
        else:
            o = i // 2
            w_in = _prep_odd_w_in(w_in_odd[o])
            w_out = w_out_odd[o].astype(BF16)
            hp, tail = _odd_prompt(hp, b, s, g, w_in, w_out, swa_sink[o], swa_slopes)
            odd_p.append(tail)
            hs, new_win = _odd_sample(hs, g, w_in, w_out, swa_sink[o], swa_slopes, state_swa_win[o])
            odd_s.append(new_win)
        ffn_w = (w_ffn_up[i].astype(BF16), w_ffn_down[i].astype(BF16),
                 w_ple_proj[i].astype(BF16), w_ple_gate[i].astype(BF16))
        hp = _ffn_and_ple(hp, p_prompt[i].reshape(b * s, -1), g, *ffn_w)
        hs = _ffn_and_ple(hs, p_sample[i].reshape(db, -1), g, *ffn_w)
    stack = lambda seq, k: jnp.stack([t[k] for t in seq])
    return (hp.reshape(b, s, d), hs.reshape(db, 1, d),
            stack(even_p, 0), stack(even_s, 0), stack(even_p, 1), stack(even_s, 1),
            stack(even_p, 2), stack(even_s, 2), stack(even_p, 3), stack(even_s, 3),
            jnp.stack(odd_p), jnp.stack(odd_s))
```

```python
import functools

import jax
import jax.numpy as jnp
from jax import lax
from jax.experimental import pallas as pl
from jax.experimental.pallas import tpu as pltpu

F32 = jnp.float32
BF16 = jnp.bfloat16

HEAD_DIM = 128
NSA_HEADS = 8
NSA_KV = 2
NSA_REP = NSA_HEADS // NSA_KV
NSA_BLOCK = 64
NSA_BLOCK_SHIFT = 6
NSA_N_SEL = 16
NSA_WINDOW = 512
NSA_FORCE = 1e4
FOX_HEADS = 8
SWA_HEADS = 32
SWA_KV = 4
SWA_REP = SWA_HEADS // SWA_KV
SWA_HEAD_DIM = 64
SWA_WINDOW = 128
PAGE_SIZE = 128
EPS = 1e-6

LANES = 128
MASKED = -1e30
LOG2E = 1.4426950408889634
LOGIT_CHUNK_VREGS = 256
VMEM_LIMIT = 56 * 1024 * 1024

E_QA = 0
E_KVA = NSA_HEADS * HEAD_DIM
E_QB = E_KVA + 6 * NSA_KV * HEAD_DIM
E_KB = E_QB + FOX_HEADS * HEAD_DIM
E_VB = E_KB + FOX_HEADS * HEAD_DIM
E_GATE = E_VB + FOX_HEADS * HEAD_DIM
E_FORGET_LANE = 3 * NSA_HEADS
E_COLS = 6144
O_Q = 0
O_K = SWA_HEADS * SWA_HEAD_DIM
O_V = O_K + SWA_KV * LANES
O_COLS = O_V + SWA_KV * LANES


def _cparams(sem):
    return pltpu.CompilerParams(dimension_semantics=sem, vmem_limit_bytes=VMEM_LIMIT)


def _pick(n, pref):
    t = min(n, pref)
    while n % t:
        t //= 2
    return t


def _rms(x, g):
    return x * lax.rsqrt(jnp.mean(x * x, axis=-1, keepdims=True) + EPS) * g


def _split3(x):
    hi = x.astype(BF16)
    r = x - hi.astype(F32)
    mid = r.astype(BF16)
    lo = (r - mid.astype(F32)).astype(BF16)
    return hi, mid, lo


def _prenorm_mm_kernel(x_ref, g_ref, w_ref, o_ref, xn_ref):
    @pl.when(pl.program_id(1) == 0)
    def _():
        xn_ref[...] = _rms(x_ref[...], g_ref[...]).astype(BF16)

    o_ref[...] = jnp.dot(xn_ref[...], w_ref[...], preferred_element_type=F32)


def prenorm_matmul(x, g, w, *, tm=512, tn=1024):
    m, k = x.shape
    n = w.shape[1]
    tm, tn = _pick(m, tm), _pick(n, tn)
    return pl.pallas_call(
        _prenorm_mm_kernel,
        out_shape=jax.ShapeDtypeStruct((m, n), F32),
        grid=(m // tm, n // tn),
        in_specs=[pl.BlockSpec((tm, k), lambda i, j: (i, 0)),
                  pl.BlockSpec((1, k), lambda i, j: (0, 0)),
                  pl.BlockSpec((k, tn), lambda i, j: (0, j))],
        out_specs=pl.BlockSpec((tm, tn), lambda i, j: (i, j)),
        scratch_shapes=[pltpu.VMEM((tm, k), BF16)],
        compiler_params=_cparams(("parallel", "arbitrary")),
        name="prenorm_matmul",
    )(x, g.reshape(1, k), w)


def _prenorm_swiglu_kernel(x_ref, g_ref, wg_ref, wu_ref, o_ref, xn_ref):
    @pl.when(pl.program_id(1) == 0)
    def _():
        xn_ref[...] = _rms(x_ref[...], g_ref[...]).astype(BF16)

    xn = xn_ref[...]
    gate = jnp.dot(xn, wg_ref[...], preferred_element_type=F32)
    up = jnp.dot(xn, wu_ref[...], preferred_element_type=F32)
    o_ref[...] = (gate * jax.nn.sigmoid(gate) * up).astype(BF16)


def prenorm_swiglu(x, g, w_up, *, tm=512, tn=512):
    m, k = x.shape
    f = w_up.shape[1] // 2
    tm, tn = _pick(m, tm), _pick(f, tn)
    nj = f // tn
    return pl.pallas_call(
        _prenorm_swiglu_kernel,
        out_shape=jax.ShapeDtypeStruct((m, f), BF16),
        grid=(m // tm, nj),
        in_specs=[pl.BlockSpec((tm, k), lambda i, j: (i, 0)),
                  pl.BlockSpec((1, k), lambda i, j: (0, 0)),
                  pl.BlockSpec((k, tn), lambda i, j: (0, j)),
                  pl.BlockSpec((k, tn), lambda i, j: (0, j + nj))],
        out_specs=pl.BlockSpec((tm, tn), lambda i, j: (i, j)),
        scratch_shapes=[pltpu.VMEM((tm, k), BF16)],
        compiler_params=_cparams(("parallel", "arbitrary")),
        name="prenorm_swiglu",
    )(x, g.reshape(1, k), w_up, w_up)


def _mmn_postnorm_res_kernel(*refs, k_sizes):
    n_in = len(k_sizes)
    a_refs, (w_ref, g_ref, h_ref, o_ref) = refs[:n_in], refs[n_in:]
    acc = None
    off = 0
    for a_ref, ks in zip(a_refs, k_sizes):
        part = jnp.dot(a_ref[...].astype(BF16), w_ref[off:off + ks, :], preferred_element_type=F32)
        acc = part if acc is None else acc + part
        off += ks
    o_ref[...] = h_ref[...] + _rms(acc, g_ref[...])


def concat_matmul_postnorm_residual(a_list, w, g, h, *, tm=256):
    m, n = h.shape
    k_sizes = tuple(a.shape[1] for a in a_list)
    tm = _pick(m, tm)
    return pl.pallas_call(
        functools.partial(_mmn_postnorm_res_kernel, k_sizes=k_sizes),
        out_shape=jax.ShapeDtypeStruct((m, n), F32),
        grid=(m // tm,),
        in_specs=[pl.BlockSpec((tm, ks), lambda i: (i, 0)) for ks in k_sizes]
        + [pl.BlockSpec(w.shape, lambda i: (0, 0)),
           pl.BlockSpec((1, n), lambda i: (0, 0)),
           pl.BlockSpec((tm, n), lambda i: (i, 0))],
        out_specs=pl.BlockSpec((tm, n), lambda i: (i, 0)),
        compiler_params=_cparams(("parallel",)),
        name="concat_matmul_postnorm_residual",
    )(*a_list, w, g.reshape(1, n), h)


def _lane_pick(x, lane):
    idx = lax.broadcasted_iota(jnp.int32, x.shape, x.ndim - 1)
    return jnp.sum(jnp.where(idx == lane, x, 0.0), axis=-1, keepdims=True)


def _nsa_cmp_kernel(slopes_ref, q_ref, ksrc_ref, vsrc_ref, pe_ref, phi_ref, gz_ref, gb_ref,
                    o_ref, sel_ref, kc_ref, vc_ref, *, tq, nb, nbp):
    g = pl.program_id(1)
    qi = pl.program_id(2)

    @pl.when(qi == 0)
    def _():
        for src, dst, c in ((ksrc_ref, kc_ref, 0), (vsrc_ref, vc_ref, 1)):
            x = src[...].reshape(nb, NSA_BLOCK, HEAD_DIM) + pe_ref[c][None]
            pooled = jnp.mean(x, axis=1)
            dst[...] = jnp.zeros_like(dst)
            dst[0:nb, :] = jnp.dot(pooled.astype(BF16), phi_ref[c].astype(BF16),
                                   preferred_element_type=F32)

    q = q_ref[...]
    q_pos = qi * tq + lax.broadcasted_iota(jnp.int32, (tq, nbp), 0)
    n_idx = lax.broadcasted_iota(jnp.int32, (tq, nbp), 1)
    complete = (n_idx + 1) * NSA_BLOCK - 1 <= q_pos
    valid = complete & (n_idx < nb)
    dist = q_pos.astype(F32) - (n_idx.astype(F32) * NSA_BLOCK + (NSA_BLOCK - 1) / 2)
    kcb = kc_ref[...].astype(BF16)
    vcb = vc_ref[...].astype(BF16)
    gates = jax.nn.sigmoid(gz_ref[...] + gb_ref[...])
    scale = HEAD_DIM ** -0.5
    imp = jnp.zeros((tq, nbp), F32)
    for r in range(NSA_REP):
        qr = q[:, r * HEAD_DIM:(r + 1) * HEAD_DIM].astype(BF16)
        s = lax.dot_general(qr, kcb, (((1,), (1,)), ((), ())), preferred_element_type=F32) * scale
        s = s - slopes_ref[g * NSA_REP + r] * dist
        s = jnp.where(valid, s, MASKED)
        m = jnp.max(s, axis=-1, keepdims=True)
        m = jnp.where(m > 0.5 * MASKED, m, 0.0)
        e = jnp.exp(s - m)
        p = e / jnp.maximum(jnp.sum(e, axis=-1, keepdims=True), 1e-30)
        imp = imp + p
        o_r = jnp.dot(p.astype(BF16), vcb, preferred_element_type=F32)
        o_ref[:, r * HEAD_DIM:(r + 1) * HEAD_DIM] = _lane_pick(gates, g * NSA_REP + r) * o_r

    cur = jnp.right_shift(q_pos, NSA_BLOCK_SHIFT)
    forced = (n_idx == 0) | (n_idx == cur) | (n_idx == cur - 1)
    score = jnp.where(forced, NSA_FORCE, jnp.where(complete, imp, -NSA_FORCE))
    score = jnp.where(n_idx < nb, score, -jnp.inf)
    rank = jnp.zeros((tq, nbp), F32)
    for i in range(nb):
        ci = score[:, i:i + 1]
        rank = rank + jnp.where((ci > score) | ((ci == score) & (n_idx > i)), 1.0, 0.0)
    sel_ref[...] = jnp.where((rank < min(NSA_N_SEL, nb)) & (n_idx < nb), 1.0, 0.0)


def nsa_compress_select(z, slopes, pe, phi, gate_b, *, tq=256):
    b, s, _ = z.shape
    nb = s // NSA_BLOCK
    assert s % NSA_BLOCK == 0 and nb % 8 == 0 and nb <= LANES
    tq = _pick(s, tq)
    cb = lambda off: off // HEAD_DIM
    kern = functools.partial(_nsa_cmp_kernel, tq=tq, nb=nb, nbp=LANES)
    return pl.pallas_call(
        kern,
        out_shape=(jax.ShapeDtypeStruct((b, s, NSA_HEADS * HEAD_DIM), F32),
                   jax.ShapeDtypeStruct((b, NSA_KV, s, LANES), F32)),
        grid=(b, NSA_KV, s // tq),
        in_specs=[pl.BlockSpec(memory_space=pltpu.SMEM),
                  pl.BlockSpec((None, tq, NSA_REP * HEAD_DIM), lambda bi, g, i: (bi, i, g)),
                  pl.BlockSpec((None, s, HEAD_DIM), lambda bi, g, i: (bi, 0, cb(E_KVA) + g)),
                  pl.BlockSpec((None, s, HEAD_DIM), lambda bi, g, i: (bi, 0, cb(E_KVA) + NSA_KV + g)),
                  pl.BlockSpec(pe.shape, lambda bi, g, i: (0, 0, 0)),
                  pl.BlockSpec(phi.shape, lambda bi, g, i: (0, 0, 0)),
                  pl.BlockSpec((None, tq, LANES), lambda bi, g, i: (bi, i, cb(E_GATE))),
                  pl.BlockSpec((1, LANES), lambda bi, g, i: (0, 0))],
        out_specs=(pl.BlockSpec((None, tq, NSA_REP * HEAD_DIM), lambda bi, g, i: (bi, i, g)),
                   pl.BlockSpec((None, None, tq, LANES), lambda bi, g, i: (bi, g, i, 0))),
        scratch_shapes=[pltpu.VMEM((LANES, HEAD_DIM), F32), pltpu.VMEM((LANES, HEAD_DIM), F32)],
        compiler_params=_cparams(("parallel", "parallel", "arbitrary")),
        name="nsa_compress_select",
    )(slopes, z, z, z, pe, phi, z, gate_b)


def _kv_tile(qi, j, *, tq, tk, window):
    q0 = qi * tq
    first = 0 if window is None else lax.div(jnp.maximum(q0 - window, 0), tk)
    last = lax.div(q0 + tq - 1, tk)
    return first + j, last


def _flash_t_kernel(*refs, mode, tq, tk, rep, nsteps, window, branch, rc):
    it = iter(refs)
    slopes_ref = next(it) if mode in ("sel", "win", "swa") else None
    sink_ref = next(it) if mode == "swa" else None
    q_ref, k_ref, v_ref = next(it), next(it), next(it)
    sel_ref = next(it) if mode == "sel" else None
    fq_ref, fk_ref = (next(it), next(it)) if mode == "fox" else (None, None)
    prev_ref, gz_ref, gb_ref = (next(it), next(it), next(it)) if mode in ("sel", "win") else (None,) * 3
    o_ref, m_ref, l_ref, acc_ref = next(it), next(it), next(it), next(it)

    g = pl.program_id(1)
    qi = pl.program_id(2)
    j = pl.program_id(3)
    kj, kv_last = _kv_tile(qi, j, tq=tq, tk=tk, window=window)
    head_dim = SWA_HEAD_DIM if mode == "swa" else HEAD_DIM
    qscale = head_dim ** -0.5 * LOG2E
    q0 = qi * tq
    n_chunks = tq // rc

    head_lanes = lambda r: slice(r * rc, (r + 1) * rc)

    @pl.when(j == 0)
    def _():
        if mode == "swa":
            for c in range(n_chunks):
                t_rel = (c * rc + lax.broadcasted_iota(jnp.int32, (1, rc), 1)).astype(F32)
                for r in range(rep):
                    m_ref[c, :, head_lanes(r)] = (sink_ref[g * rep + r]
                                                  + slopes_ref[g * rep + r] * t_rel) * LOG2E
            l_ref[...] = jnp.ones_like(l_ref)
        else:
            m_ref[...] = jnp.full(m_ref.shape, MASKED, F32)
            l_ref[...] = jnp.zeros_like(l_ref)
        acc_ref[...] = jnp.zeros_like(acc_ref)

    @pl.when(kj <= kv_last)
    def _():
        k = k_ref[...].astype(BF16)
        v_t = v_ref[...].T.astype(BF16)
        k0 = kj * tk
        diff = lax.broadcasted_iota(jnp.int32, (tk, rc), 1) - lax.broadcasted_iota(jnp.int32, (tk, rc), 0)
        if mode == "sel":
            blk_of_key = jnp.right_shift(k0 + lax.broadcasted_iota(jnp.int32, (tk, LANES), 0),
                                         NSA_BLOCK_SHIFT)
            onehot = jnp.where(blk_of_key == lax.broadcasted_iota(jnp.int32, (tk, LANES), 1), 1.0, 0.0)
            k = jnp.concatenate([k, onehot.astype(BF16)], axis=1)
            penalty = jnp.where(sel_ref[...] > 0.5, 0.0, MASKED).astype(BF16)
        if mode == "fox":
            key_bias = (fq_ref[:, 0:1] - _lane_pick(fk_ref[...], E_FORGET_LANE + g)) * LOG2E
        else:
            key_pos = (lax.broadcasted_iota(jnp.int32, (tk, 1), 0) + (k0 - q0)).astype(F32)

        def query(r, rows):
            if mode == "swa":
                qp = q_ref[rows, (r // 2) * LANES:(r // 2 + 1) * LANES]
                upper = lax.broadcasted_iota(jnp.int32, qp.shape, 1) >= SWA_HEAD_DIM
                qr = jnp.where(upper if r % 2 else ~upper, qp, 0.0)
            else:
                qr = q_ref[rows, r * HEAD_DIM:(r + 1) * HEAD_DIM]
            qr = (qr * qscale).astype(BF16)
            if mode == "sel":
                qr = jnp.concatenate([qr, penalty[rows]], axis=1)
            return qr

        for c in range(n_chunks):
            rows = slice(c * rc, (c + 1) * rc)
            shift = q0 + c * rc - k0
            mask = diff >= -shift
            if window is not None:
                mask = mask & (diff <= window - shift)
            q_all = jnp.concatenate([query(r, rows) for r in range(rep)], axis=0)
            s_all = lax.dot_general(k, q_all, (((1,), (1,)), ((), ())), preferred_element_type=F32)
            parts = []
            for r in range(rep):
                bias = key_bias if mode == "fox" else (slopes_ref[g * rep + r] * LOG2E) * key_pos
                parts.append(jnp.where(mask, s_all[:, head_lanes(r)] + bias, MASKED))
            s = jnp.concatenate(parts, axis=1)
            m_prev = m_ref[c]
            m_new = jnp.maximum(m_prev, jnp.max(s, axis=0, keepdims=True))
            alpha = jnp.exp2(m_prev - m_new)
            p = jnp.exp2(s - m_new)
            l_ref[c] = alpha * l_ref[c] + jnp.sum(p, axis=0, keepdims=True)
            acc_ref[c] = alpha * acc_ref[c] + jnp.dot(v_t, p.astype(BF16), preferred_element_type=F32)
            m_ref[c] = m_new

    @pl.when(j == nsteps - 1)
    def _():
        if mode in ("sel", "win"):
            gates = jax.nn.sigmoid(gz_ref[...] + gb_ref[...])
        for c in range(n_chunks):
            rows = slice(c * rc, (c + 1) * rc)
            o_t = acc_ref[c] / l_ref[c]
            out = lambda r: o_t[:, head_lanes(r)].T
            for r in range(rep):
                blk = slice(r * HEAD_DIM, (r + 1) * HEAD_DIM)
                if mode in ("sel", "win"):
                    gate = _lane_pick(gates[rows], branch * NSA_HEADS + g * rep + r)
                    o_ref[rows, blk] = prev_ref[rows, blk] + gate * out(r)
                elif mode == "fox":
                    o_ref[rows, :] = out(r)
                elif r % 2:
                    o = out(r)
                    upper = lax.broadcasted_iota(jnp.int32, o.shape, 1) >= SWA_HEAD_DIM
                    o_ref[rows, (r // 2) * LANES:(r // 2 + 1) * LANES] = jnp.where(upper, o, out(r - 1))


def _flash_call(mode, z, *, tq, tk, rep, n_kv, window, branch, q_col, k_col, v_col, out_cols,
                smem=(), extra=(), extra_specs=()):
    b, s, _ = z.shape
    tq, tk = _pick(s, tq), _pick(s, tk)
    nq = s // tq

    def first_last(qi):
        first = 0 if window is None else max(qi * tq - window, 0) // tk
        return first, (qi * tq + tq - 1) // tk

    nsteps = max(l - f + 1 for f, l in map(first_last, range(nq)))
    qw = rep * HEAD_DIM if mode != "swa" else rep * SWA_HEAD_DIM

    def kv_map(col):
        def index(bi, g, i, j):
            kj, last = _kv_tile(i, j, tq=tq, tk=tk, window=window)
            return bi, jnp.minimum(kj, last), col + g
        return index

    in_specs = [pl.BlockSpec(memory_space=pltpu.SMEM) for _ in smem]
    in_specs += [pl.BlockSpec((None, tq, qw), lambda bi, g, i, j: (bi, i, q_col + g)),
                 pl.BlockSpec((None, tk, LANES), kv_map(k_col)),
                 pl.BlockSpec((None, tk, LANES), kv_map(v_col))]
    in_specs += list(extra_specs(tq, tk, kv_map))
    rc = min(tq, max(LANES, LOGIT_CHUNK_VREGS * 8 * LANES // (tk * rep)))
    kern = functools.partial(_flash_t_kernel, mode=mode, tq=tq, tk=tk, rep=rep, nsteps=nsteps,
                             window=window, branch=branch, rc=rc)
    return pl.pallas_call(
        kern,
        out_shape=jax.ShapeDtypeStruct((b, s, out_cols), F32),
        grid=(b, n_kv, nq, nsteps),
        in_specs=in_specs,
        out_specs=pl.BlockSpec((None, tq, qw), lambda bi, g, i, j: (bi, i, g)),
        scratch_shapes=[pltpu.VMEM((tq // rc, 1, rep * rc), F32), pltpu.VMEM((tq // rc, 1, rep * rc), F32),
                        pltpu.VMEM((tq // rc, LANES, rep * rc), F32)],
        compiler_params=_cparams(("parallel", "parallel", "parallel", "arbitrary")),
        name="flash_" + mode,
    )(*smem, z, z, z, *extra)


def _nsa_gate_specs(tq):
    return [pl.BlockSpec((None, tq, NSA_REP * HEAD_DIM), lambda bi, g, i, j: (bi, i, g)),
            pl.BlockSpec((None, tq, LANES), lambda bi, g, i, j: (bi, i, E_GATE // LANES)),
            pl.BlockSpec((1, LANES), lambda bi, g, i, j: (0, 0))]


def nsa_selected(z, sel, prev, slopes, gate_b, *, t=512):
    def specs(tq, tk, kv_map):
        return [pl.BlockSpec((None, None, tq, LANES), lambda bi, g, i, j: (bi, g, i, 0))] + _nsa_gate_specs(tq)
    kvc = E_KVA // HEAD_DIM
    return _flash_call("sel", z, tq=t, tk=t, rep=NSA_REP, n_kv=NSA_KV, window=None, branch=1,
                       q_col=0, k_col=kvc + 2 * NSA_KV, v_col=kvc + 3 * NSA_KV,
                       out_cols=NSA_HEADS * HEAD_DIM, smem=(slopes,), extra=(sel, prev, z, gate_b),
                       extra_specs=specs)


def nsa_window(z, prev, slopes, gate_b, *, t=512):
    def specs(tq, tk, kv_map):
        return _nsa_gate_specs(tq)
    kvc = E_KVA // HEAD_DIM
    return _flash_call("win", z, tq=t, tk=t, rep=NSA_REP, n_kv=NSA_KV, window=NSA_WINDOW, branch=2,
                       q_col=0, k_col=kvc + 4 * NSA_KV, v_col=kvc + 5 * NSA_KV,
                       out_cols=NSA_HEADS * HEAD_DIM, smem=(slopes,), extra=(prev, z, gate_b),
                       extra_specs=specs)


def fox_attention(z, f_cum, f_cum_t, *, t=512):
    def specs(tq, tk, kv_map):
        def fk_map(bi, g, i, j):
            return (bi, kv_map(0)(bi, g, i, j)[1], 0)
        return [pl.BlockSpec((None, None, 1, tq), lambda bi, g, i, j: (bi, g, 0, i)),
                pl.BlockSpec((None, tk, LANES), fk_map)]
    return _flash_call("fox", z, tq=t, tk=t, rep=1, n_kv=FOX_HEADS, window=None, branch=0,
                       q_col=E_QB // HEAD_DIM, k_col=E_KB // HEAD_DIM, v_col=E_VB // HEAD_DIM,
                       out_cols=FOX_HEADS * HEAD_DIM, extra=(f_cum_t, f_cum), extra_specs=specs)


def swa_attention(z, slopes, sink, *, tq=256, tk=128):
    return _flash_call("swa", z, tq=tq, tk=tk, rep=SWA_REP, n_kv=SWA_KV, window=SWA_WINDOW, branch=0,
                       q_col=0, k_col=O_K // LANES, v_col=O_V // LANES,
                       out_cols=SWA_HEADS * SWA_HEAD_DIM, smem=(slopes, sink),
                       extra_specs=lambda tq, tk, kv_map: [])


def _fox_gates_kernel(gz_ref, fb_ref, logf_ref, f_ref, ft_ref, *, chunk):
    x = gz_ref[...] + fb_ref[...]
    logf = jnp.minimum(x, 0.0) - jnp.log1p(jnp.exp(-jnp.abs(x)))
    logf_ref[...] = logf
    tri = jnp.where(lax.broadcasted_iota(jnp.int32, (chunk, chunk), 0)
                    >= lax.broadcasted_iota(jnp.int32, (chunk, chunk), 1), 1.0, 0.0).astype(BF16)
    carry = jnp.zeros((1, LANES), F32)
    for c in range(logf.shape[0] // chunk):
        rows = slice(c * chunk, (c + 1) * chunk)
        y = carry
        for part in _split3(logf[rows]):
            y = y + jnp.dot(tri, part, preferred_element_type=F32)
        f_ref[rows, :] = y
        carry = y[chunk - 1:chunk, :]
    ft_ref[...] = f_ref[...].T


def fox_gates(z, forget_b_lanes, *, chunk=256):
    b, s, _ = z.shape
    chunk = _pick(s, chunk)
    shp = jax.ShapeDtypeStruct((b, s, LANES), F32)
    return pl.pallas_call(
        functools.partial(_fox_gates_kernel, chunk=chunk),
        out_shape=(shp, shp, jax.ShapeDtypeStruct((b, LANES, s), F32)),
        grid=(b,),
        in_specs=[pl.BlockSpec((None, s, LANES), lambda bi: (bi, 0, E_GATE // LANES)),
                  pl.BlockSpec((1, LANES), lambda bi: (0, 0))],
        out_specs=(pl.BlockSpec((None, s, LANES), lambda bi: (bi, 0, 0)),
                   pl.BlockSpec((None, s, LANES), lambda bi: (bi, 0, 0)),
                   pl.BlockSpec((None, LANES, s), lambda bi: (bi, 0, 0))),
        compiler_params=_cparams(("parallel",)),
        name="fox_gates",
    )(z, forget_b_lanes)


def _alibi_slopes(n):
    return 2.0 ** (-8.0 * jnp.arange(1, n + 1, dtype=F32) / n)


def _prep_even_w_in(w):
    d = w.shape[0]
    n_gate = 3 * NSA_HEADS
    src_gate = E_QB
    src_qb = src_gate + n_gate
    src_forget = src_qb + 3 * FOX_HEADS * HEAD_DIM
    used = E_GATE + n_gate + FOX_HEADS
    return jnp.concatenate([w[:, :src_gate], w[:, src_qb:src_forget], w[:, src_gate:src_qb],
                            w[:, src_forget:], jnp.zeros((d, E_COLS - used), w.dtype)], axis=1).astype(BF16)


def _prep_odd_w_in(w):
    d = w.shape[0]
    kw = SWA_KV * SWA_HEAD_DIM
    k = w[:, O_K:O_K + kw].reshape(d, SWA_KV, SWA_HEAD_DIM)
    v = w[:, O_K + kw:].reshape(d, SWA_KV, SWA_HEAD_DIM)
    dup = lambda x: jnp.concatenate([x, x], axis=-1).reshape(d, SWA_KV * LANES)
    return jnp.concatenate([w[:, :O_K], dup(k), dup(v)], axis=1).astype(BF16)


def _lane_row(values, offset):
    return jnp.zeros((1, LANES), F32).at[0, offset:offset + values.shape[0]].set(values)


def _ffn_down_ple_kernel(a_ref, wd_ref, g3_ref, h_ref, p_ref, wpp_ref, wpg_ref, g4_ref, o_ref):
    y = jnp.dot(a_ref[...], wd_ref[...], preferred_element_type=F32)
    h = h_ref[...] + _rms(y, g3_ref[...])
    proj = jnp.dot(p_ref[...].astype(BF16), wpp_ref[...], preferred_element_type=F32)
    gate = jnp.dot(h.astype(BF16), wpg_ref[...], preferred_element_type=F32)
    o_ref[...] = h + _rms(proj, g4_ref[...]) * jax.nn.sigmoid(gate)


def ffn_down_ple(act, w_down, g3, h, p, w_pp, w_pg, g4, *, tm=256):
    m, d = h.shape
    f = act.shape[1]
    pd = p.shape[1]
    tm = _pick(m, tm)
    row = lambda width: pl.BlockSpec((tm, width), lambda i: (i, 0))
    whole = lambda shape: pl.BlockSpec(shape, lambda i: (0, 0))
    return pl.pallas_call(
        _ffn_down_ple_kernel,
        out_shape=jax.ShapeDtypeStruct((m, d), F32),
        grid=(m // tm,),
        in_specs=[row(f), whole((f, d)), whole((1, d)), row(d), row(pd), whole((pd, d)), whole((d, d)),
                  whole((1, d))],
        out_specs=row(d),
        compiler_params=_cparams(("parallel",)),
        name="ffn_down_ple",
    )(act, w_down, g3.reshape(1, d), h, p, w_pp, w_pg, g4.reshape(1, d))


def _ffn_and_ple(h, p, g, w_up, w_down, w_pp, w_pg):
    act = prenorm_swiglu(h, g[2], w_up)
    return ffn_down_ple(act, w_down, g[3], h, p, w_pp, w_pg, g[4])


def _even_prompt(h, b, s, g, w_in, w_out, gate_b, forget_b, pe, phi, slopes):
    z = prenorm_matmul(h, g[0], w_in).reshape(b, s, E_COLS)
    o_a, sel = nsa_compress_select(z, slopes, pe, phi, gate_b)
    o_a = nsa_selected(z, sel, o_a, slopes, gate_b)
    o_a = nsa_window(z, o_a, slopes, gate_b)
    logf, f_cum, f_cum_t = fox_gates(z, forget_b)
    fl = slice(E_FORGET_LANE, E_FORGET_LANE + FOX_HEADS)
    o_b = fox_attention(z, f_cum, f_cum_t[:, fl, :].reshape(b, FOX_HEADS, 1, s))
    m = b * s
    h = concat_matmul_postnorm_residual([o_a.reshape(m, -1), o_b.reshape(m, -1)], w_out, g[1], h)
    wb = min(NSA_WINDOW, s)
    win_off = E_KVA + 4 * NSA_KV * HEAD_DIM
    outs = (z[:, :, E_KVA:win_off].reshape(b, s, 4, NSA_KV, HEAD_DIM),
            z[:, :, E_KB:E_GATE].reshape(b, s, 2, FOX_HEADS, HEAD_DIM),
            logf[:, :, fl],
            z[:, s - wb:, win_off:E_QB].reshape(b, wb, 2, NSA_KV, HEAD_DIM))
    return h, outs


def _odd_prompt(h, b, s, g, w_in, w_out, sink, slopes):
    z = prenorm_matmul(h, g[0], w_in).reshape(b, s, O_COLS)
    o = swa_attention(z, slopes, sink)
    h = concat_matmul_postnorm_residual([o.reshape(b * s, -1)], w_out, g[1], h)
    wb = min(SWA_WINDOW, s)
    tail = z[:, s - wb:, O_K:].reshape(b, wb, 2, SWA_KV, 2, SWA_HEAD_DIM)[:, :, :, :, 0]
    return h, tail


NSA_POOL_PAGES = 16
FOX_PAGES = 16


def _nsa_pool_kernel(pt_ref, *refs, n_pages):
    page_refs, (pe_ref, o_ref) = refs[:n_pages], refs[n_pages:]
    blocks_per_page = PAGE_SIZE // NSA_BLOCK
    for i, ref in enumerate(page_refs):
        x = ref[...].reshape((blocks_per_page, NSA_BLOCK) + ref.shape[1:]) + pe_ref[...][None]
        o_ref[i * blocks_per_page:(i + 1) * blocks_per_page] = jnp.mean(x, axis=1)


def nsa_pool_pages(cache, page_table, pe_slabs):
    db, n_pages = page_table.shape
    slabs = cache.shape[2:]
    p = _pick(n_pages, NSA_POOL_PAGES)
    bpp = PAGE_SIZE // NSA_BLOCK

    def page_map(i):
        return lambda b, c, pt: (pt[b, c * p + i], 0, 0, 0)

    return pl.pallas_call(
        functools.partial(_nsa_pool_kernel, n_pages=p),
        out_shape=jax.ShapeDtypeStruct((db, n_pages * bpp) + slabs, F32),
        grid_spec=pltpu.PrefetchScalarGridSpec(
            num_scalar_prefetch=1,
            grid=(db, n_pages // p),
            in_specs=[pl.BlockSpec((PAGE_SIZE, None) + slabs, page_map(i)) for i in range(p)]
            + [pl.BlockSpec((NSA_BLOCK,) + slabs, lambda b, c, pt: (0, 0, 0))],
            out_specs=pl.BlockSpec((None, p * bpp) + slabs, lambda b, c, pt: (b, c, 0, 0))),
        compiler_params=_cparams(("parallel", "arbitrary")),
        name="nsa_pool_pages",
    )(page_table, *([cache] * p), pe_slabs)


def _transposed_rows(row):
    n = row.shape[1]
    bc = jnp.broadcast_to(row, (n, n))
    return bc.T, bc


def _nsa_sample_cmp_kernel(q_ref, pooled_ref, phi_ref, slopes_ref, gz_ref, gb_ref, o_ref, idx_ref,
                           *, q_pos, n_pick):
    n_slabs = 2 * NSA_KV
    npb = pooled_ref.shape[0] // n_slabs
    q = q_ref[...].astype(BF16)
    slab = lambda s: pooled_ref[pl.ds(s, npb, stride=n_slabs), :].astype(BF16)
    n_row = lax.broadcasted_iota(jnp.int32, (1, npb), 1)
    complete = (n_row + 1) * NSA_BLOCK - 1 <= q_pos
    dist = q_pos - (n_row.astype(F32) * NSA_BLOCK + (NSA_BLOCK - 1) / 2)
    head_row = lax.broadcasted_iota(jnp.int32, (NSA_HEADS, 1), 0)
    gates = jax.nn.sigmoid(gz_ref[...] + gb_ref[...])
    lane = lax.broadcasted_iota(jnp.int32, (NSA_HEADS, LANES), 1)
    gate_cmp = jnp.sum(jnp.where(lane == head_row, gates, 0.0), axis=-1, keepdims=True)
    scale = HEAD_DIM ** -0.5
    i_idx = lax.broadcasted_iota(jnp.int32, (npb, npb), 0)
    j_idx = lax.broadcasted_iota(jnp.int32, (npb, npb), 1)
    o_acc = jnp.zeros((NSA_HEADS, HEAD_DIM), F32)
    for g in range(NSA_KV):
        in_group = (head_row >= g * NSA_REP) & (head_row < (g + 1) * NSA_REP)
        kc = jnp.dot(slab(g), phi_ref[0].astype(BF16), preferred_element_type=F32).astype(BF16)
        vc = jnp.dot(slab(NSA_KV + g), phi_ref[1].astype(BF16), preferred_element_type=F32).astype(BF16)
        s = lax.dot_general(q, kc, (((1,), (1,)), ((), ())), preferred_element_type=F32) * scale
        s = s - slopes_ref[...] * dist
        s = jnp.where(complete, s, MASKED)
        m = jnp.max(s, axis=-1, keepdims=True)
        m = jnp.where(m > 0.5 * MASKED, m, 0.0)
        e = jnp.exp(s - m)
        p = e / jnp.maximum(jnp.sum(e, axis=-1, keepdims=True), 1e-30)
        o_g = jnp.dot(p.astype(BF16), vc, preferred_element_type=F32)
        o_acc = jnp.where(in_group, o_g, o_acc)
        imp = jnp.sum(jnp.where(in_group, p, 0.0), axis=0, keepdims=True)

        cur = q_pos // NSA_BLOCK
        forced = (n_row == 0) | (n_row == cur) | (n_row == cur - 1)
        score = jnp.where(forced, NSA_FORCE, jnp.where(complete, imp, -NSA_FORCE))
        col, bc = _transposed_rows(score)
        rank = jnp.sum(jnp.where((col > bc) | ((col == bc) & (i_idx < j_idx)), 1.0, 0.0),
                       axis=0, keepdims=True)
        sel = jnp.where(rank < n_pick, 1.0, 0.0)
        sel_col, _ = _transposed_rows(sel)
        pos = jnp.sum(jnp.where(i_idx < j_idx, sel_col, 0.0), axis=0, keepdims=True)
        slot = lax.broadcasted_iota(jnp.int32, (NSA_N_SEL, npb), 0).astype(F32)
        n_f = lax.broadcasted_iota(jnp.int32, (NSA_N_SEL, npb), 1).astype(F32)
        hit = (sel > 0.5) & (pos == slot)
        ids = jnp.sum(jnp.where(hit, n_f, 0.0), axis=-1, keepdims=True)
        idx_ref[g] = jnp.broadcast_to(ids, (NSA_N_SEL, LANES)).astype(jnp.int32)
    o_ref[...] = gate_cmp * o_acc


def nsa_sample_compress_select(q, pooled, phi, slopes, gz, gate_b, *, q_pos, n_pick):
    db = q.shape[0]
    pooled = pooled.reshape(db, -1, HEAD_DIM)
    npb = pooled.shape[1]
    kern = functools.partial(_nsa_sample_cmp_kernel, q_pos=q_pos, n_pick=n_pick)
    return pl.pallas_call(
        kern,
        out_shape=(jax.ShapeDtypeStruct((db, NSA_HEADS, HEAD_DIM), F32),
                   jax.ShapeDtypeStruct((db, NSA_KV, NSA_N_SEL, LANES), jnp.int32)),
        grid=(db,),
        in_specs=[pl.BlockSpec((None, NSA_HEADS, HEAD_DIM), lambda b: (b, 0, 0)),
                  pl.BlockSpec((None, npb, pooled.shape[2]), lambda b: (b, 0, 0)),
                  pl.BlockSpec(phi.shape, lambda b: (0, 0, 0)),
                  pl.BlockSpec((NSA_HEADS, 1), lambda b: (0, 0)),
                  pl.BlockSpec((None, 1, LANES), lambda b: (b, 0, 0)),
                  pl.BlockSpec((1, LANES), lambda b: (0, 0))],
        out_specs=(pl.BlockSpec((None, NSA_HEADS, HEAD_DIM), lambda b: (b, 0, 0)),
                   pl.BlockSpec((None, NSA_KV, NSA_N_SEL, LANES), lambda b: (b, 0, 0, 0))),
        compiler_params=_cparams(("parallel",)),
        name="nsa_sample_compress_select",
    )(q, pooled, phi, slopes.reshape(NSA_HEADS, 1), gz, gate_b)


def _nsa_sample_attend_kernel(idx_ref, pt_ref, *refs, n_pick, q_pos):
    n_blk = NSA_KV * n_pick
    n_slabs = 4 * NSA_KV
    blk_refs = refs[:n_blk]
    q_ref, new_ref, win_ref, prev_ref, slopes_ref, gz_ref, gb_ref, o_ref, win_out_ref = refs[n_blk:]
    b = pl.program_id(0)
    w = win_ref.shape[0]
    scale = HEAD_DIM ** -0.5
    gates = jax.nn.sigmoid(gz_ref[...] + gb_ref[...])
    r_col = lax.broadcasted_iota(jnp.int32, (NSA_REP, 1), 0)
    lane = lax.broadcasted_iota(jnp.int32, (NSA_REP, LANES), 1)
    row_w = lax.broadcasted_iota(jnp.int32, (w, HEAD_DIM), 0)
    n_keys = n_pick * NSA_BLOCK
    key_lane = lax.broadcasted_iota(jnp.int32, (1, n_keys), 1)

    def rounded(x):
        return x.astype(BF16).astype(F32)

    def attend(q4, keys, vals, k_new, v_new, dist, valid, slope):
        s = lax.dot_general(q4.astype(BF16), keys.astype(BF16), (((1,), (1,)), ((), ())),
                            preferred_element_type=F32) * scale - slope * dist
        s = jnp.where(valid, s, MASKED)
        s_new = jnp.sum(rounded(q4) * rounded(k_new), axis=-1, keepdims=True) * scale
        m = jnp.maximum(jnp.max(s, axis=-1, keepdims=True), s_new)
        e = jnp.exp(s - m)
        e_new = jnp.exp(s_new - m)
        den = jnp.sum(e, axis=-1, keepdims=True) + e_new
        num = jnp.dot(e.astype(BF16), vals.astype(BF16), preferred_element_type=F32)
        return (num + rounded(e_new) * rounded(v_new)) / den

    for g in range(NSA_KV):
        q4 = q_ref[g]
        slope = slopes_ref[g]
        new = lambda c: new_ref[c * NSA_KV + g:c * NSA_KV + g + 1, :]
        gate = lambda c: jnp.sum(jnp.where(lane == c * NSA_HEADS + g * NSA_REP + r_col, gates, 0.0),
                                 axis=-1, keepdims=True)
        blocks = blk_refs[g * n_pick:(g + 1) * n_pick]
        slab = lambda r, c: r[pl.ds(c * NSA_KV + g, NSA_BLOCK, stride=n_slabs), :]
        keys = jnp.concatenate([slab(r, 2) for r in blocks], axis=0)
        vals = jnp.concatenate([slab(r, 3) for r in blocks], axis=0)
        k_pos = key_lane & (NSA_BLOCK - 1)
        for i in range(n_pick):
            in_blk = jnp.right_shift(key_lane, NSA_BLOCK_SHIFT) == i
            k_pos = k_pos + jnp.where(in_blk, idx_ref[b, g, i] * NSA_BLOCK, 0)
        dist = q_pos - k_pos
        o_sel = attend(q4, keys, vals, new(2), new(3), dist.astype(F32), dist >= 0, slope)
        k_w = win_ref[:, g * HEAD_DIM:(g + 1) * HEAD_DIM]
        v_w = win_ref[:, (NSA_KV + g) * HEAD_DIM:(NSA_KV + g + 1) * HEAD_DIM]
        dist_w = w - lax.broadcasted_iota(jnp.int32, (1, w), 1)
        o_win = attend(q4, k_w, v_w, new(4), new(5), dist_w.astype(F32), dist_w <= NSA_WINDOW, slope)
        o_ref[g] = prev_ref[g] + gate(1) * o_sel + gate(2) * o_win
        for c, buf in ((0, k_w), (1, v_w)):
            rolled = pltpu.roll(buf, shift=w - 1, axis=0)
            col = (c * NSA_KV + g) * HEAD_DIM
            win_out_ref[:, col:col + HEAD_DIM] = jnp.where(row_w == w - 1, new(4 + c), rolled)


def nsa_sample_attend(cache, idx, page_table, q, new_rows, win, prev, slopes, gz, gate_b, *, n_pick, q_pos):
    db = q.shape[0]
    w = win.shape[1]
    bpp = PAGE_SIZE // NSA_BLOCK
    bpp_shift = bpp.bit_length() - 1

    def blk_map(g, i):
        def index(b, idx_r, pt_r):
            blk = idx_r[b, g, i]
            return pt_r[b, jnp.right_shift(blk, bpp_shift)] * bpp + (blk & (bpp - 1)), 0
        return index

    blk_specs = [pl.BlockSpec((NSA_BLOCK * 4 * NSA_KV, HEAD_DIM), blk_map(g, i))
                 for g in range(NSA_KV) for i in range(n_pick)]
    full = lambda shape: pl.BlockSpec(shape, lambda b, idx_r, pt_r: (0,) * len(shape))
    per_b = lambda shape: pl.BlockSpec((None,) + shape, lambda b, idx_r, pt_r: (b,) + (0,) * len(shape))
    kern = functools.partial(_nsa_sample_attend_kernel, n_pick=n_pick, q_pos=q_pos)
    return pl.pallas_call(
        kern,
        out_shape=(jax.ShapeDtypeStruct((db, NSA_KV, NSA_REP, HEAD_DIM), F32),
                   jax.ShapeDtypeStruct(win.shape, F32)),
        grid_spec=pltpu.PrefetchScalarGridSpec(
            num_scalar_prefetch=2,
            grid=(db,),
            in_specs=blk_specs + [per_b((NSA_KV, NSA_REP, HEAD_DIM)), per_b(new_rows.shape[1:]),
                                  per_b(win.shape[1:]), per_b((NSA_KV, NSA_REP, HEAD_DIM)),
                                  full((NSA_KV, NSA_REP, 1)), per_b((1, LANES)), full((1, LANES))],
            out_specs=(per_b((NSA_KV, NSA_REP, HEAD_DIM)), per_b(win.shape[1:]))),
        compiler_params=_cparams(("arbitrary",)),
        name="nsa_sample_attend",
    )(idx, page_table, *([cache] * len(blk_specs)), q, new_rows, win, prev,
      slopes.reshape(NSA_KV, NSA_REP, 1), gz, gate_b)


def _fox_decode_kernel(pt_ref, *refs, n_pages):
    kv_refs, lf_refs = refs[:n_pages], refs[n_pages:2 * n_pages]
    (q_ref, knew_ref, vnew_ref, vrow_ref, fraw_ref, fb_ref,
     o_ref, lfnew_ref, qbd_ref, m_ref, l_ref, acc_ref, carry_ref) = refs[2 * n_pages:]
    c = pl.program_id(1)
    width = FOX_HEADS * HEAD_DIM
    scale = HEAD_DIM ** -0.5
    head_row = lax.broadcasted_iota(jnp.int32, (FOX_HEADS, width), 0)
    head_of_lane = jnp.right_shift(lax.broadcasted_iota(jnp.int32, (FOX_HEADS, width), 1),
                                   HEAD_DIM.bit_length() - 1)

    @pl.when(c == 0)
    def _():
        q = q_ref[...]
        qbd_ref[...] = jnp.where(head_of_lane == head_row, jnp.concatenate([q] * FOX_HEADS, axis=1),
                                 0.0).astype(BF16)
        x = fraw_ref[...] + fb_ref[...]
        lf_new = jnp.minimum(x, 0.0) - jnp.log1p(jnp.exp(-jnp.abs(x)))
        lfnew_ref[...] = lf_new
        carry_ref[...] = lf_new
        rounded = lambda t: t.astype(BF16).astype(F32)
        m_ref[...] = jnp.sum(rounded(q) * rounded(knew_ref[...]), axis=-1, keepdims=True) * scale
        l_ref[...] = jnp.ones_like(l_ref)
        acc_ref[...] = jnp.broadcast_to(rounded(vrow_ref[...]), acc_ref.shape)

    later = jnp.where(lax.broadcasted_iota(jnp.int32, (PAGE_SIZE, PAGE_SIZE), 0)
                      > lax.broadcasted_iota(jnp.int32, (PAGE_SIZE, PAGE_SIZE), 1), 1.0, 0.0).astype(BF16)
    lf_all = jnp.concatenate([lf_ref[...] for lf_ref in lf_refs], axis=0)
    rows = lf_all.shape[0]
    suffix3 = jnp.dot(jnp.concatenate(_split3(lf_all), axis=0), later, preferred_element_type=F32)
    suffix = suffix3[:rows] + suffix3[rows:2 * rows] + suffix3[2 * rows:]
    totals = jnp.sum(lf_all, axis=-1, keepdims=True)
    carry = carry_ref[...]
    biases = []
    for i in range(n_pages):
        page_rows = slice(i * FOX_HEADS, (i + 1) * FOX_HEADS)
        biases.append(carry + suffix[page_rows])
        carry = carry + totals[page_rows]
    carry_ref[...] = carry
    bias = jnp.concatenate(biases, axis=1)
    def heads_on_lanes(kv_ref, first):
        return jnp.concatenate([kv_ref[pl.ds(first + h, PAGE_SIZE, stride=2 * FOX_HEADS), :].astype(BF16)
                                for h in range(FOX_HEADS)], axis=1)

    kp = jnp.concatenate([heads_on_lanes(kv_ref, 0) for kv_ref in kv_refs], axis=0)
    vp = jnp.concatenate([heads_on_lanes(kv_ref, FOX_HEADS) for kv_ref in kv_refs], axis=0)
    s = lax.dot_general(qbd_ref[...], kp, (((1,), (1,)), ((), ())),
                        preferred_element_type=F32) * scale + bias
    m_prev = m_ref[...]
    m_new = jnp.maximum(m_prev, jnp.max(s, axis=-1, keepdims=True))
    alpha = jnp.exp(m_prev - m_new)
    p = jnp.exp(s - m_new)
    l_ref[...] = alpha * l_ref[...] + jnp.sum(p, axis=-1, keepdims=True)
    acc_ref[...] = alpha * acc_ref[...] + jnp.dot(p.astype(BF16), vp, preferred_element_type=F32)
    m_ref[...] = m_new

    @pl.when(c == pl.num_programs(1) - 1)
    def _():
        acc = jnp.where(head_of_lane == head_row, acc_ref[...], 0.0)
        o = acc[:, :HEAD_DIM]
        for h in range(1, FOX_HEADS):
            o = o + acc[:, h * HEAD_DIM:(h + 1) * HEAD_DIM]
        o_ref[...] = o / l_ref[...]


def fox_decode(cache_kv, cache_logf_t, page_table, q, k_new, v_new, f_raw, forget_b):
    db, n_pages = page_table.shape
    p = _pick(n_pages, FOX_PAGES)
    width = FOX_HEADS * HEAD_DIM
    page_rows = PAGE_SIZE * 2 * FOX_HEADS

    def page_map(i):
        return lambda b, c, pt: (pt[b, n_pages - 1 - (c * p + i)], 0, 0)

    def kv_page_map(i):
        return lambda b, c, pt: (pt[b, n_pages - 1 - (c * p + i)], 0)

    per_b = lambda shape: pl.BlockSpec((None,) + shape, lambda b, c, pt: (b,) + (0,) * len(shape))
    return pl.pallas_call(
        functools.partial(_fox_decode_kernel, n_pages=p),
        out_shape=(jax.ShapeDtypeStruct((db, FOX_HEADS, HEAD_DIM), F32),
                   jax.ShapeDtypeStruct((db, FOX_HEADS, 1), F32)),
        grid_spec=pltpu.PrefetchScalarGridSpec(
            num_scalar_prefetch=1,
            grid=(db, n_pages // p),
            in_specs=[pl.BlockSpec((page_rows, HEAD_DIM), kv_page_map(i)) for i in range(p)]
            + [pl.BlockSpec((None, FOX_HEADS, PAGE_SIZE), page_map(i)) for i in range(p)]
            + [per_b((FOX_HEADS, HEAD_DIM)), per_b((FOX_HEADS, HEAD_DIM)), per_b((FOX_HEADS, HEAD_DIM)),
               per_b((1, width)), per_b((FOX_HEADS, 1)),
               pl.BlockSpec((FOX_HEADS, 1), lambda b, c, pt: (0, 0))],
            out_specs=(per_b((FOX_HEADS, HEAD_DIM)), per_b((FOX_HEADS, 1))),
            scratch_shapes=[pltpu.VMEM((FOX_HEADS, width), BF16), pltpu.VMEM((FOX_HEADS, 1), F32),
                            pltpu.VMEM((FOX_HEADS, 1), F32), pltpu.VMEM((FOX_HEADS, width), F32),
                            pltpu.VMEM((FOX_HEADS, 1), F32)]),
        compiler_params=_cparams(("parallel", "arbitrary")),
        name="fox_decode",
    )(page_table, *([cache_kv] * p), *([cache_logf_t] * p), q, k_new, v_new,
      v_new.reshape(db, 1, width), f_raw, forget_b.reshape(FOX_HEADS, 1))


def _swa_decode_kernel(q_ref, new_ref, win_ref, slopes_ref, sink_ref, o_ref, win_out_ref):
    w = win_ref.shape[0]
    kvw = SWA_KV * SWA_HEAD_DIM
    scale = SWA_HEAD_DIM ** -0.5
    rounded = lambda t: t.astype(BF16).astype(F32)
    dist = (w - lax.broadcasted_iota(jnp.int32, (1, w), 1))
    new = new_ref[...]
    for g in range(SWA_KV):
        q = q_ref[g]
        lo = g * SWA_HEAD_DIM
        k_w = win_ref[:, lo:lo + SWA_HEAD_DIM]
        v_w = win_ref[:, kvw + lo:kvw + lo + SWA_HEAD_DIM]
        k_new = new[:, lo:lo + SWA_HEAD_DIM]
        v_new = new[:, kvw + lo:kvw + lo + SWA_HEAD_DIM]
        s = lax.dot_general(q.astype(BF16), k_w.astype(BF16), (((1,), (1,)), ((), ())),
                            preferred_element_type=F32) * scale - slopes_ref[g] * dist.astype(F32)
        s = jnp.where(dist <= SWA_WINDOW, s, MASKED)
        s_new = jnp.sum(rounded(q) * rounded(k_new), axis=-1, keepdims=True) * scale
        sink = sink_ref[g]
        m = jnp.maximum(jnp.maximum(jnp.max(s, axis=-1, keepdims=True), s_new), sink)
        e = jnp.exp(s - m)
        e_new = jnp.exp(s_new - m)
        den = jnp.sum(e, axis=-1, keepdims=True) + e_new + jnp.exp(sink - m)
        num = jnp.dot(e.astype(BF16), v_w.astype(BF16), preferred_element_type=F32)
        o_ref[g] = (num + rounded(e_new) * rounded(v_new)) / den
    buf = win_ref[...]
    row = lax.broadcasted_iota(jnp.int32, buf.shape, 0)
    win_out_ref[...] = jnp.where(row == w - 1, new, pltpu.roll(buf, shift=w - 1, axis=0))


def swa_decode(q, new_row, win, slopes, sink):
    db = q.shape[0]
    per_b = lambda shape: pl.BlockSpec((None,) + shape, lambda b: (b,) + (0,) * len(shape))
    full = lambda shape: pl.BlockSpec(shape, lambda b: (0,) * len(shape))
    return pl.pallas_call(
        _swa_decode_kernel,
        out_shape=(jax.ShapeDtypeStruct(q.shape, F32), jax.ShapeDtypeStruct(win.shape, F32)),
        grid=(db,),
        in_specs=[per_b(q.shape[1:]), per_b(new_row.shape[1:]), per_b(win.shape[1:]),
                  full((SWA_KV, SWA_REP, 1)), full((SWA_KV, SWA_REP, 1))],
        out_specs=(per_b(q.shape[1:]), per_b(win.shape[1:])),
        compiler_params=_cparams(("parallel",)),
        name="swa_decode",
    )(q, new_row, win, slopes.reshape(SWA_KV, SWA_REP, 1), sink.reshape(SWA_KV, SWA_REP, 1))


def _even_sample(h, g, w_in, w_out, gate_b, forget_b, pe, phi, slopes,
                 cache_nsa, cache_fox_kv, cache_fox_logf, win_buf, page_table, e):
    db = h.shape[0]
    past = page_table.shape[1] * PAGE_SIZE
    n_pick = NSA_N_SEL - 1
    assert past % NSA_BLOCK == 0 and past // NSA_BLOCK >= n_pick
    n_even, pool = cache_nsa.shape[:2]
    page_table = page_table + e * pool
    nsa_rows = cache_nsa.reshape(-1, HEAD_DIM)
    nsa_halves = cache_nsa.reshape(-1, 2, 2 * NSA_KV, HEAD_DIM)
    fox_rows = cache_fox_kv.reshape(-1, HEAD_DIM)
    logf_t = cache_fox_logf.reshape((n_even * pool,) + cache_fox_logf.shape[2:]).transpose(0, 2, 1)
    win_buf = win_buf.reshape((n_even * db,) + win_buf.shape[2:])[e * db:(e + 1) * db]
    z = prenorm_matmul(h, g[0], w_in)
    heads = lambda lo, hi: z[:, lo:hi].reshape(db, -1, HEAD_DIM)
    gz = z[:, E_GATE:E_GATE + LANES].reshape(db, 1, LANES)
    q_a = heads(E_QA, E_KVA)
    pe_slabs = jnp.stack([pe[0]] * NSA_KV + [pe[1]] * NSA_KV, axis=1)
    pooled = nsa_pool_pages(nsa_halves, page_table, pe_slabs)
    o_cmp, idx = nsa_sample_compress_select(q_a, pooled, phi, slopes, gz, gate_b, q_pos=past, n_pick=n_pick)
    grp = lambda x: x.reshape(db, NSA_KV, NSA_REP, HEAD_DIM)
    w = win_buf.shape[1]
    o_a, new_win = nsa_sample_attend(nsa_rows, idx[..., 0], page_table, grp(q_a), heads(E_KVA, E_QB),
                                     win_buf.reshape(db, w, -1), grp(o_cmp), slopes, gz, gate_b,
                                     n_pick=n_pick, q_pos=past)
    o_b, lf_new = fox_decode(fox_rows, logf_t,
                             page_table, heads(E_QB, E_KB), heads(E_KB, E_VB), heads(E_VB, E_GATE),
                             z[:, E_GATE + E_FORGET_LANE:E_GATE + E_FORGET_LANE + FOX_HEADS].reshape(db, -1, 1),
                             forget_b)
    h = concat_matmul_postnorm_residual([o_a.reshape(db, -1), o_b.reshape(db, -1)], w_out, g[1], h)
    win_off = E_KVA + 4 * NSA_KV * HEAD_DIM
    outs = (z[:, E_KVA:win_off].reshape(db, 1, 4, NSA_KV, HEAD_DIM),
            z[:, E_KB:E_GATE].reshape(db, 1, 2, FOX_HEADS, HEAD_DIM),
            lf_new.reshape(db, 1, FOX_HEADS),
            new_win.reshape(win_buf.shape))
    return h, outs


def _odd_sample(h, g, w_in, w_out, sink, slopes, win_buf):
    db = h.shape[0]
    w = win_buf.shape[1]
    z = prenorm_matmul(h, g[0], w_in)
    q = z[:, :O_K].reshape(db, SWA_KV, SWA_REP, SWA_HEAD_DIM)
    new_row = z[:, O_K:].reshape(db, 2, SWA_KV, 2, SWA_HEAD_DIM)[:, :, :, 0].reshape(db, 1, -1)
    o, new_win = swa_decode(q, new_row, win_buf.reshape(db, w, -1), slopes, sink)
    h = concat_matmul_postnorm_residual([o.reshape(db, -1)], w_out, g[1], h)
    return h, new_win.reshape(win_buf.shape)


def kernel(x_prompt, x_sample, cache_nsa_kv, cache_fox_kv, cache_fox_logf, state_nsa_win, state_swa_win,
           page_table, p_prompt, p_sample, w_in_even, w_out_even, nsa_gate_b, nsa_pe, nsa_phi, fox_forget_b,
           w_in_odd, w_out_odd, swa_sink, norm_g, w_ffn_up, w_ffn_down, w_ple_proj, w_ple_gate):
    b, s, d = x_prompt.shape
    db, dec_seq, _ = x_sample.shape
    assert dec_seq == 1, "the decode kernels handle one new token per sequence"
    depth = norm_g.shape[0]
    nsa_slopes = _alibi_slopes(NSA_HEADS)
    swa_slopes = _alibi_slopes(SWA_HEADS)
    hp = x_prompt.reshape(b * s, d)
    hs = x_sample.reshape(db, d)
    even_p, even_s, odd_p, odd_s = [], [], [], []
    for i in range(depth):
        g = norm_g[i]
        if i % 2 == 0:
            e = i // 2
            w_in = _prep_even_w_in(w_in_even[e])
            w_out = w_out_even[e].astype(BF16)
            gate_b = _lane_row(nsa_gate_b[e], 0)
            hp, outs = _even_prompt(hp, b, s, g, w_in, w_out, gate_b,
                                    _lane_row(fox_forget_b[e], E_FORGET_LANE), nsa_pe[e], nsa_phi[e], nsa_slopes)
            even_p.append(outs)
            hs, outs = _even_sample(hs, g, w_in, w_out, gate_b, fox_forget_b[e], nsa_pe[e], nsa_phi[e],
                                    nsa_slopes, cache_nsa_kv, cache_fox_kv, cache_fox_logf,
                                    state_nsa_win, page_table, e)
            even_s.append(outs)
        else:
            o = i // 2
            w_in = _prep_odd_w_in(w_in_odd[o])
            w_out = w_out_odd[o].astype(BF16)
            hp, tail = _odd_prompt(hp, b, s, g, w_in, w_out, swa_sink[o], swa_slopes)
            odd_p.append(tail)
            hs, new_win = _odd_sample(hs, g, w_in, w_out, swa_sink[o], swa_slopes, state_swa_win[o])
            odd_s.append(new_win)
        ffn_w = (w_ffn_up[i].astype(BF16), w_ffn_down[i].astype(BF16),
                 w_ple_proj[i].astype(BF16), w_ple_gate[i].astype(BF16))
        hp = _ffn_and_ple(hp, p_prompt[i].reshape(b * s, -1), g, *ffn_w)
        hs = _ffn_and_ple(hs, p_sample[i].reshape(db, -1), g, *ffn_w)
    stack = lambda seq, k: jnp.stack([t[k] for t in seq])
    return (hp.reshape(b, s, d), hs.reshape(db, 1, d),
            stack(even_p, 0), stack(even_s, 0), stack(even_p, 1), stack(even_s, 1),
            stack(even_p, 2), stack(even_s, 2), stack(even_p, 3), stack(even_s, 3),
            jnp.stack(odd_p), jnp.stack(odd_s))
```

```python
import functools

import jax
import jax.numpy as jnp
from jax import lax
from jax.experimental import pallas as pl
from jax.experimental.pallas import tpu as pltpu

F32 = jnp.float32
BF16 = jnp.bfloat16

HEAD_DIM = 128
NSA_HEADS = 8
NSA_KV = 2
NSA_REP = NSA_HEADS // NSA_KV
NSA_BLOCK = 64
NSA_BLOCK_SHIFT = 6
NSA_N_SEL = 16
NSA_WINDOW = 512
NSA_FORCE = 1e4
FOX_HEADS = 8
SWA_HEADS = 32
SWA_KV = 4
SWA_REP = SWA_HEADS // SWA_KV
SWA_HEAD_DIM = 64
SWA_WINDOW = 128
PAGE_SIZE = 128
EPS = 1e-6

LANES = 128
MASKED = -1e30
LOG2E = 1.4426950408889634
LOGIT_CHUNK_VREGS = 256
VMEM_LIMIT = 56 * 1024 * 1024

E_QA = 0
E_KVA = NSA_HEADS * HEAD_DIM
E_QB = E_KVA + 6 * NSA_KV * HEAD_DIM
E_KB = E_QB + FOX_HEADS * HEAD_DIM
E_VB = E_KB + FOX_HEADS * HEAD_DIM
E_GATE = E_VB + FOX_HEADS * HEAD_DIM
E_FORGET_LANE = 3 * NSA_HEADS
E_COLS = 6144
O_Q = 0
O_K = SWA_HEADS * SWA_HEAD_DIM
O_V = O_K + SWA_KV * LANES
O_COLS = O_V + SWA_KV * LANES


def _cparams(sem):
    return pltpu.CompilerParams(dimension_semantics=sem, vmem_limit_bytes=VMEM_LIMIT)


def _pick(n, pref):
    t = min(n, pref)
    while n % t:
        t //= 2
    return t


def _rms(x, g):
    return x * lax.rsqrt(jnp.mean(x * x, axis=-1, keepdims=True) + EPS) * g


def _split3(x):
    hi = x.astype(BF16)
    r = x - hi.astype(F32)
    mid = r.astype(BF16)
    lo = (r - mid.astype(F32)).astype(BF16)
    return hi, mid, lo


def _prenorm_mm_kernel(x_ref, g_ref, w_ref, o_ref, xn_ref):
    @pl.when(pl.program_id(1) == 0)
    def _():
        xn_ref[...] = _rms(x_ref[...], g_ref[...]).astype(BF16)

    o_ref[...] = jnp.dot(xn_ref[...], w_ref[...], preferred_element_type=F32)


def prenorm_matmul(x, g, w, *, tm=512, tn=1024):
    m, k = x.shape
    n = w.shape[1]
    tm, tn = _pick(m, tm), _pick(n, tn)
    return pl.pallas_call(
        _prenorm_mm_kernel,
        out_shape=jax.ShapeDtypeStruct((m, n), F32),
        grid=(m // tm, n // tn),
        in_specs=[pl.BlockSpec((tm, k), lambda i, j: (i, 0)),
                  pl.BlockSpec((1, k), lambda i, j: (0, 0)),
                  pl.BlockSpec((k, tn), lambda i, j: (0, j))],
        out_specs=pl.BlockSpec((tm, tn), lambda i, j: (i, j)),
        scratch_shapes=[pltpu.VMEM((tm, k), BF16)],
        compiler_params=_cparams(("parallel", "arbitrary")),
        name="prenorm_matmul",
    )(x, g.reshape(1, k), w)


def _prenorm_swiglu_kernel(x_ref, g_ref, wg_ref, wu_ref, o_ref, xn_ref):
    @pl.when(pl.program_id(1) == 0)
    def _():
        xn_ref[...] = _rms(x_ref[...], g_ref[...]).astype(BF16)

    xn = xn_ref[...]
    gate = jnp.dot(xn, wg_ref[...].astype(BF16), preferred_element_type=F32)
    up = jnp.dot(xn, wu_ref[...].astype(BF16), preferred_element_type=F32)
    o_ref[...] = (gate * jax.nn.sigmoid(gate) * up).astype(BF16)


def prenorm_swiglu(x, g, w_up, layer, *, tm=512, tn=512):
    m, k = x.shape
    f = w_up.shape[2] // 2
    tm, tn = _pick(m, tm), _pick(f, tn)
    nj = f // tn
    return pl.pallas_call(
        _prenorm_swiglu_kernel,
        out_shape=jax.ShapeDtypeStruct((m, f), BF16),
        grid=(m // tm, nj),
        in_specs=[pl.BlockSpec((tm, k), lambda i, j: (i, 0)),
                  pl.BlockSpec((1, k), lambda i, j: (0, 0)),
                  pl.BlockSpec((None, k, tn), lambda i, j: (layer, 0, j)),
                  pl.BlockSpec((None, k, tn), lambda i, j: (layer, 0, j + nj))],
        out_specs=pl.BlockSpec((tm, tn), lambda i, j: (i, j)),
        scratch_shapes=[pltpu.VMEM((tm, k), BF16)],
        compiler_params=_cparams(("parallel", "arbitrary")),
        name="prenorm_swiglu",
    )(x, g.reshape(1, k), w_up, w_up)


def _mmn_postnorm_res_kernel(*refs, k_sizes):
    n_in = len(k_sizes)
    a_refs, (w_ref, g_ref, h_ref, o_ref) = refs[:n_in], refs[n_in:]
    acc = None
    off = 0
    for a_ref, ks in zip(a_refs, k_sizes):
        part = jnp.dot(a_ref[...].astype(BF16), w_ref[off:off + ks, :], preferred_element_type=F32)
        acc = part if acc is None else acc + part
        off += ks
    o_ref[...] = h_ref[...] + _rms(acc, g_ref[...])


def concat_matmul_postnorm_residual(a_list, w, g, h, *, tm=256):
    m, n = h.shape
    k_sizes = tuple(a.shape[1] for a in a_list)
    tm = _pick(m, tm)
    return pl.pallas_call(
        functools.partial(_mmn_postnorm_res_kernel, k_sizes=k_sizes),
        out_shape=jax.ShapeDtypeStruct((m, n), F32),
        grid=(m // tm,),
        in_specs=[pl.BlockSpec((tm, ks), lambda i: (i, 0)) for ks in k_sizes]
        + [pl.BlockSpec(w.shape, lambda i: (0, 0)),
           pl.BlockSpec((1, n), lambda i: (0, 0)),
           pl.BlockSpec((tm, n), lambda i: (i, 0))],
        out_specs=pl.BlockSpec((tm, n), lambda i: (i, 0)),
        compiler_params=_cparams(("parallel",)),
        name="concat_matmul_postnorm_residual",
    )(*a_list, w, g.reshape(1, n), h)


def _lane_pick(x, lane):
    idx = lax.broadcasted_iota(jnp.int32, x.shape, x.ndim - 1)
    return jnp.sum(jnp.where(idx == lane, x, 0.0), axis=-1, keepdims=True)


def _nsa_cmp_kernel(slopes_ref, q_ref, ksrc_ref, vsrc_ref, pe_ref, phi_ref, gz_ref, gb_ref,
                    o_ref, sel_ref, kc_ref, vc_ref, *, tq, nb, nbp):
    g = pl.program_id(1)
    qi = pl.program_id(2)

    @pl.when(qi == 0)
    def _():
        for src, dst, c in ((ksrc_ref, kc_ref, 0), (vsrc_ref, vc_ref, 1)):
            x = src[...].reshape(nb, NSA_BLOCK, HEAD_DIM) + pe_ref[c][None]
            pooled = jnp.mean(x, axis=1)
            dst[...] = jnp.zeros_like(dst)
            dst[0:nb, :] = jnp.dot(pooled.astype(BF16), phi_ref[c].astype(BF16),
                                   preferred_element_type=F32)

    q = q_ref[...]
    q_pos = qi * tq + lax.broadcasted_iota(jnp.int32, (tq, nbp), 0)
    n_idx = lax.broadcasted_iota(jnp.int32, (tq, nbp), 1)
    complete = (n_idx + 1) * NSA_BLOCK - 1 <= q_pos
    valid = complete & (n_idx < nb)
    dist = q_pos.astype(F32) - (n_idx.astype(F32) * NSA_BLOCK + (NSA_BLOCK - 1) / 2)
    kcb = kc_ref[...].astype(BF16)
    vcb = vc_ref[...].astype(BF16)
    gates = jax.nn.sigmoid(gz_ref[...] + gb_ref[...])
    scale = HEAD_DIM ** -0.5
    imp = jnp.zeros((tq, nbp), F32)
    for r in range(NSA_REP):
        qr = q[:, r * HEAD_DIM:(r + 1) * HEAD_DIM].astype(BF16)
        s = lax.dot_general(qr, kcb, (((1,), (1,)), ((), ())), preferred_element_type=F32) * scale
        s = s - slopes_ref[g * NSA_REP + r] * dist
        s = jnp.where(valid, s, MASKED)
        m = jnp.max(s, axis=-1, keepdims=True)
        m = jnp.where(m > 0.5 * MASKED, m, 0.0)
        e = jnp.exp(s - m)
        p = e / jnp.maximum(jnp.sum(e, axis=-1, keepdims=True), 1e-30)
        imp = imp + p
        o_r = jnp.dot(p.astype(BF16), vcb, preferred_element_type=F32)
        o_ref[:, r * HEAD_DIM:(r + 1) * HEAD_DIM] = _lane_pick(gates, g * NSA_REP + r) * o_r

    cur = jnp.right_shift(q_pos, NSA_BLOCK_SHIFT)
    forced = (n_idx == 0) | (n_idx == cur) | (n_idx == cur - 1)
    score = jnp.where(forced, NSA_FORCE, jnp.where(complete, imp, -NSA_FORCE))
    score = jnp.where(n_idx < nb, score, -jnp.inf)
    rank = jnp.zeros((tq, nbp), F32)
    for i in range(nb):
        ci = score[:, i:i + 1]
        rank = rank + jnp.where((ci > score) | ((ci == score) & (n_idx > i)), 1.0, 0.0)
    sel_ref[...] = jnp.where((rank < min(NSA_N_SEL, nb)) & (n_idx < nb), 1.0, 0.0)


def nsa_compress_select(z, slopes, pe, phi, gate_b, *, tq=256):
    b, s, _ = z.shape
    nb = s // NSA_BLOCK
    assert s % NSA_BLOCK == 0 and nb % 8 == 0 and nb <= LANES
    tq = _pick(s, tq)
    cb = lambda off: off // HEAD_DIM
    kern = functools.partial(_nsa_cmp_kernel, tq=tq, nb=nb, nbp=LANES)
    return pl.pallas_call(
        kern,
        out_shape=(jax.ShapeDtypeStruct((b, s, NSA_HEADS * HEAD_DIM), F32),
                   jax.ShapeDtypeStruct((b, NSA_KV, s, LANES), F32)),
        grid=(b, NSA_KV, s // tq),
        in_specs=[pl.BlockSpec(memory_space=pltpu.SMEM),
                  pl.BlockSpec((None, tq, NSA_REP * HEAD_DIM), lambda bi, g, i: (bi, i, g)),
                  pl.BlockSpec((None, s, HEAD_DIM), lambda bi, g, i: (bi, 0, cb(E_KVA) + g)),
                  pl.BlockSpec((None, s, HEAD_DIM), lambda bi, g, i: (bi, 0, cb(E_KVA) + NSA_KV + g)),
                  pl.BlockSpec(pe.shape, lambda bi, g, i: (0, 0, 0)),
                  pl.BlockSpec(phi.shape, lambda bi, g, i: (0, 0, 0)),
                  pl.BlockSpec((None, tq, LANES), lambda bi, g, i: (bi, i, cb(E_GATE))),
                  pl.BlockSpec((1, LANES), lambda bi, g, i: (0, 0))],
        out_specs=(pl.BlockSpec((None, tq, NSA_REP * HEAD_DIM), lambda bi, g, i: (bi, i, g)),
                   pl.BlockSpec((None, None, tq, LANES), lambda bi, g, i: (bi, g, i, 0))),
        scratch_shapes=[pltpu.VMEM((LANES, HEAD_DIM), F32), pltpu.VMEM((LANES, HEAD_DIM), F32)],
        compiler_params=_cparams(("parallel", "parallel", "arbitrary")),
        name="nsa_compress_select",
    )(slopes, z, z, z, pe, phi, z, gate_b)


def _kv_tile(qi, j, *, tq, tk, window):
    q0 = qi * tq
    first = 0 if window is None else lax.div(jnp.maximum(q0 - window, 0), tk)
    last = lax.div(q0 + tq - 1, tk)
    return first + j, last


def _flash_t_kernel(*refs, mode, tq, tk, rep, nsteps, window, branch, rc):
    it = iter(refs)
    slopes_ref = next(it) if mode in ("sel", "win", "swa") else None
    sink_ref = next(it) if mode == "swa" else None
    q_ref, k_ref, v_ref = next(it), next(it), next(it)
    sel_ref = next(it) if mode == "sel" else None
    fq_ref, fk_ref = (next(it), next(it)) if mode == "fox" else (None, None)
    prev_ref, gz_ref, gb_ref = (next(it), next(it), next(it)) if mode in ("sel", "win") else (None,) * 3
    o_ref, m_ref, l_ref, acc_ref = next(it), next(it), next(it), next(it)

    g = pl.program_id(1)
    qi = pl.program_id(2)
    j = pl.program_id(3)
    kj, kv_last = _kv_tile(qi, j, tq=tq, tk=tk, window=window)
    head_dim = SWA_HEAD_DIM if mode == "swa" else HEAD_DIM
    qscale = head_dim ** -0.5 * LOG2E
    q0 = qi * tq
    n_chunks = tq // rc

    head_lanes = lambda r: slice(r * rc, (r + 1) * rc)

    @pl.when(j == 0)
    def _():
        if mode == "swa":
            for c in range(n_chunks):
                t_rel = (c * rc + lax.broadcasted_iota(jnp.int32, (1, rc), 1)).astype(F32)
                for r in range(rep):
                    m_ref[c, :, head_lanes(r)] = (sink_ref[g * rep + r]
                                                  + slopes_ref[g * rep + r] * t_rel) * LOG2E
            l_ref[...] = jnp.ones_like(l_ref)
        else:
            m_ref[...] = jnp.full(m_ref.shape, MASKED, F32)
            l_ref[...] = jnp.zeros_like(l_ref)
        acc_ref[...] = jnp.zeros_like(acc_ref)

    @pl.when(kj <= kv_last)
    def _():
        k = k_ref[...].astype(BF16)
        v_t = v_ref[...].T.astype(BF16)
        k0 = kj * tk
        diff = lax.broadcasted_iota(jnp.int32, (tk, rc), 1) - lax.broadcasted_iota(jnp.int32, (tk, rc), 0)
        if mode == "sel":
            blk_of_key = jnp.right_shift(k0 + lax.broadcasted_iota(jnp.int32, (tk, LANES), 0),
                                         NSA_BLOCK_SHIFT)
            onehot = jnp.where(blk_of_key == lax.broadcasted_iota(jnp.int32, (tk, LANES), 1), 1.0, 0.0)
            k = jnp.concatenate([k, onehot.astype(BF16)], axis=1)
            penalty = jnp.where(sel_ref[...] > 0.5, 0.0, MASKED).astype(BF16)
        if mode == "fox":
            key_bias = (fq_ref[:, 0:1] - _lane_pick(fk_ref[...], E_FORGET_LANE + g)) * LOG2E
        else:
            key_pos = (lax.broadcasted_iota(jnp.int32, (tk, 1), 0) + (k0 - q0)).astype(F32)

        def query(r, rows):
            if mode == "swa":
                qp = q_ref[rows, (r // 2) * LANES:(r // 2 + 1) * LANES]
                upper = lax.broadcasted_iota(jnp.int32, qp.shape, 1) >= SWA_HEAD_DIM
                qr = jnp.where(upper if r % 2 else ~upper, qp, 0.0)
            else:
                qr = q_ref[rows, r * HEAD_DIM:(r + 1) * HEAD_DIM]
            qr = (qr * qscale).astype(BF16)
            if mode == "sel":
                qr = jnp.concatenate([qr, penalty[rows]], axis=1)
            return qr

        for c in range(n_chunks):
            rows = slice(c * rc, (c + 1) * rc)
            shift = q0 + c * rc - k0
            mask = diff >= -shift
            if window is not None:
                mask = mask & (diff <= window - shift)
            q_all = jnp.concatenate([query(r, rows) for r in range(rep)], axis=0)
            s_all = lax.dot_general(k, q_all, (((1,), (1,)), ((), ())), preferred_element_type=F32)
            parts = []
            for r in range(rep):
                bias = key_bias if mode == "fox" else (slopes_ref[g * rep + r] * LOG2E) * key_pos
                parts.append(jnp.where(mask, s_all[:, head_lanes(r)] + bias, MASKED))
            s = jnp.concatenate(parts, axis=1)
            m_prev = m_ref[c]
            m_new = jnp.maximum(m_prev, jnp.max(s, axis=0, keepdims=True))
            alpha = jnp.exp2(m_prev - m_new)
            p = jnp.exp2(s - m_new)
            l_ref[c] = alpha * l_ref[c] + jnp.sum(p, axis=0, keepdims=True)
            acc_ref[c] = alpha * acc_ref[c] + jnp.dot(v_t, p.astype(BF16), preferred_element_type=F32)
            m_ref[c] = m_new

    @pl.when(j == nsteps - 1)
    def _():
        if mode in ("sel", "win"):
            gates = jax.nn.sigmoid(gz_ref[...] + gb_ref[...])
        for c in range(n_chunks):
            rows = slice(c * rc, (c + 1) * rc)
            o_t = acc_ref[c] / l_ref[c]
            out = lambda r: o_t[:, head_lanes(r)].T
            for r in range(rep):
                blk = slice(r * HEAD_DIM, (r + 1) * HEAD_DIM)
                if mode in ("sel", "win"):
                    gate = _lane_pick(gates[rows], branch * NSA_HEADS + g * rep + r)
                    o_ref[rows, blk] = prev_ref[rows, blk] + gate * out(r)
                elif mode == "fox":
                    o_ref[rows, :] = out(r)
                elif r % 2:
                    o = out(r)
                    upper = lax.broadcasted_iota(jnp.int32, o.shape, 1) >= SWA_HEAD_DIM
                    o_ref[rows, (r // 2) * LANES:(r // 2 + 1) * LANES] = jnp.where(upper, o, out(r - 1))


def _flash_call(mode, z, *, tq, tk, rep, n_kv, window, branch, q_col, k_col, v_col, out_cols,
                smem=(), extra=(), extra_specs=()):
    b, s, _ = z.shape
    tq, tk = _pick(s, tq), _pick(s, tk)
    nq = s // tq

    def first_last(qi):
        first = 0 if window is None else max(qi * tq - window, 0) // tk
        return first, (qi * tq + tq - 1) // tk

    nsteps = max(l - f + 1 for f, l in map(first_last, range(nq)))
    qw = rep * HEAD_DIM if mode != "swa" else rep * SWA_HEAD_DIM

    def kv_map(col):
        def index(bi, g, i, j):
            kj, last = _kv_tile(i, j, tq=tq, tk=tk, window=window)
            return bi, jnp.minimum(kj, last), col + g
        return index

    in_specs = [pl.BlockSpec(memory_space=pltpu.SMEM) for _ in smem]
    in_specs += [pl.BlockSpec((None, tq, qw), lambda bi, g, i, j: (bi, i, q_col + g)),
                 pl.BlockSpec((None, tk, LANES), kv_map(k_col)),
                 pl.BlockSpec((None, tk, LANES), kv_map(v_col))]
    in_specs += list(extra_specs(tq, tk, kv_map))
    rc = min(tq, max(LANES, LOGIT_CHUNK_VREGS * 8 * LANES // (tk * rep)))
    kern = functools.partial(_flash_t_kernel, mode=mode, tq=tq, tk=tk, rep=rep, nsteps=nsteps,
                             window=window, branch=branch, rc=rc)
    return pl.pallas_call(
        kern,
        out_shape=jax.ShapeDtypeStruct((b, s, out_cols), F32),
        grid=(b, n_kv, nq, nsteps),
        in_specs=in_specs,
        out_specs=pl.BlockSpec((None, tq, qw), lambda bi, g, i, j: (bi, i, g)),
        scratch_shapes=[pltpu.VMEM((tq // rc, 1, rep * rc), F32), pltpu.VMEM((tq // rc, 1, rep * rc), F32),
                        pltpu.VMEM((tq // rc, LANES, rep * rc), F32)],
        compiler_params=_cparams(("parallel", "parallel", "parallel", "arbitrary")),
        name="flash_" + mode,
    )(*smem, z, z, z, *extra)


def _nsa_gate_specs(tq):
    return [pl.BlockSpec((None, tq, NSA_REP * HEAD_DIM), lambda bi, g, i, j: (bi, i, g)),
            pl.BlockSpec((None, tq, LANES), lambda bi, g, i, j: (bi, i, E_GATE // LANES)),
            pl.BlockSpec((1, LANES), lambda bi, g, i, j: (0, 0))]


def nsa_selected(z, sel, prev, slopes, gate_b, *, t=512):
    def specs(tq, tk, kv_map):
        return [pl.BlockSpec((None, None, tq, LANES), lambda bi, g, i, j: (bi, g, i, 0))] + _nsa_gate_specs(tq)
    kvc = E_KVA // HEAD_DIM
    return _flash_call("sel", z, tq=t, tk=t, rep=NSA_REP, n_kv=NSA_KV, window=None, branch=1,
                       q_col=0, k_col=kvc + 2 * NSA_KV, v_col=kvc + 3 * NSA_KV,
                       out_cols=NSA_HEADS * HEAD_DIM, smem=(slopes,), extra=(sel, prev, z, gate_b),
                       extra_specs=specs)


def nsa_window(z, prev, slopes, gate_b, *, t=512):
    def specs(tq, tk, kv_map):
        return _nsa_gate_specs(tq)
    kvc = E_KVA // HEAD_DIM
    return _flash_call("win", z, tq=t, tk=t, rep=NSA_REP, n_kv=NSA_KV, window=NSA_WINDOW, branch=2,
                       q_col=0, k_col=kvc + 4 * NSA_KV, v_col=kvc + 5 * NSA_KV,
                       out_cols=NSA_HEADS * HEAD_DIM, smem=(slopes,), extra=(prev, z, gate_b),
                       extra_specs=specs)


def fox_attention(z, f_cum, f_cum_t, *, t=512):
    def specs(tq, tk, kv_map):
        def fk_map(bi, g, i, j):
            return (bi, kv_map(0)(bi, g, i, j)[1], 0)
        return [pl.BlockSpec((None, None, 1, tq), lambda bi, g, i, j: (bi, g, 0, i)),
                pl.BlockSpec((None, tk, LANES), fk_map)]
    return _flash_call("fox", z, tq=t, tk=2 * t, rep=1, n_kv=FOX_HEADS, window=None, branch=0,
                       q_col=E_QB // HEAD_DIM, k_col=E_KB // HEAD_DIM, v_col=E_VB // HEAD_DIM,
                       out_cols=FOX_HEADS * HEAD_DIM, extra=(f_cum_t, f_cum), extra_specs=specs)


def swa_attention(z, slopes, sink, *, tq=256, tk=128):
    return _flash_call("swa", z, tq=tq, tk=tk, rep=SWA_REP, n_kv=SWA_KV, window=SWA_WINDOW, branch=0,
                       q_col=0, k_col=O_K // LANES, v_col=O_V // LANES,
                       out_cols=SWA_HEADS * SWA_HEAD_DIM, smem=(slopes, sink),
                       extra_specs=lambda tq, tk, kv_map: [])


def _fox_gates_kernel(gz_ref, fb_ref, logf_ref, f_ref, ft_ref, *, chunk):
    x = gz_ref[...] + fb_ref[...]
    logf = jnp.minimum(x, 0.0) - jnp.log1p(jnp.exp(-jnp.abs(x)))
    logf_ref[...] = logf
    tri = jnp.where(lax.broadcasted_iota(jnp.int32, (chunk, chunk), 0)
                    >= lax.broadcasted_iota(jnp.int32, (chunk, chunk), 1), 1.0, 0.0).astype(BF16)
    carry = jnp.zeros((1, LANES), F32)
    for c in range(logf.shape[0] // chunk):
        rows = slice(c * chunk, (c + 1) * chunk)
        y = carry
        for part in _split3(logf[rows]):
            y = y + jnp.dot(tri, part, preferred_element_type=F32)
        f_ref[rows, :] = y
        carry = y[chunk - 1:chunk, :]
    ft_ref[...] = f_ref[...].T


def fox_gates(z, forget_b_lanes, *, chunk=256):
    b, s, _ = z.shape
    chunk = _pick(s, chunk)
    shp = jax.ShapeDtypeStruct((b, s, LANES), F32)
    return pl.pallas_call(
        functools.partial(_fox_gates_kernel, chunk=chunk),
        out_shape=(shp, shp, jax.ShapeDtypeStruct((b, LANES, s), F32)),
        grid=(b,),
        in_specs=[pl.BlockSpec((None, s, LANES), lambda bi: (bi, 0, E_GATE // LANES)),
                  pl.BlockSpec((1, LANES), lambda bi: (0, 0))],
        out_specs=(pl.BlockSpec((None, s, LANES), lambda bi: (bi, 0, 0)),
                   pl.BlockSpec((None, s, LANES), lambda bi: (bi, 0, 0)),
                   pl.BlockSpec((None, LANES, s), lambda bi: (bi, 0, 0))),
        compiler_params=_cparams(("parallel",)),
        name="fox_gates",
    )(z, forget_b_lanes)


def _alibi_slopes(n):
    return 2.0 ** (-8.0 * jnp.arange(1, n + 1, dtype=F32) / n)


def _prep_even_w_in(w):
    d = w.shape[0]
    n_gate = 3 * NSA_HEADS
    src_gate = E_QB
    src_qb = src_gate + n_gate
    src_forget = src_qb + 3 * FOX_HEADS * HEAD_DIM
    used = E_GATE + n_gate + FOX_HEADS
    return jnp.concatenate([w[:, :src_gate], w[:, src_qb:src_forget], w[:, src_gate:src_qb],
                            w[:, src_forget:], jnp.zeros((d, E_COLS - used), w.dtype)], axis=1).astype(BF16)


def _prep_odd_w_in(w):
    d = w.shape[0]
    kw = SWA_KV * SWA_HEAD_DIM
    k = w[:, O_K:O_K + kw].reshape(d, SWA_KV, SWA_HEAD_DIM)
    v = w[:, O_K + kw:].reshape(d, SWA_KV, SWA_HEAD_DIM)
    dup = lambda x: jnp.concatenate([x, x], axis=-1).reshape(d, SWA_KV * LANES)
    return jnp.concatenate([w[:, :O_K], dup(k), dup(v)], axis=1).astype(BF16)


def _lane_row(values, offset):
    return jnp.zeros((1, LANES), F32).at[0, offset:offset + values.shape[0]].set(values)


def _ffn_down_ple_kernel(a_ref, wd_ref, g3_ref, h_ref, p_ref, wpp_ref, wpg_ref, g4_ref, o_ref):
    y = jnp.dot(a_ref[...], wd_ref[...], preferred_element_type=F32)
    h = h_ref[...] + _rms(y, g3_ref[...])
    proj = jnp.dot(p_ref[...].astype(BF16), wpp_ref[...], preferred_element_type=F32)
    gate = jnp.dot(h.astype(BF16), wpg_ref[...], preferred_element_type=F32)
    o_ref[...] = h + _rms(proj, g4_ref[...]) * jax.nn.sigmoid(gate)


def ffn_down_ple(act, w_down, g3, h, p, w_pp, w_pg, g4, *, tm=256):
    m, d = h.shape
    f = act.shape[1]
    pd = p.shape[1]
    tm = _pick(m, tm)
    row = lambda width: pl.BlockSpec((tm, width), lambda i: (i, 0))
    whole = lambda shape: pl.BlockSpec(shape, lambda i: (0, 0))
    return pl.pallas_call(
        _ffn_down_ple_kernel,
        out_shape=jax.ShapeDtypeStruct((m, d), F32),
        grid=(m // tm,),
        in_specs=[row(f), whole((f, d)), whole((1, d)), row(d), row(pd), whole((pd, d)), whole((d, d)),
                  whole((1, d))],
        out_specs=row(d),
        compiler_params=_cparams(("parallel",)),
        name="ffn_down_ple",
    )(act, w_down, g3.reshape(1, d), h, p, w_pp, w_pg, g4.reshape(1, d))


def _ffn_and_ple(h, p, g, layer, w_up, w_down, w_pp, w_pg):
    act = prenorm_swiglu(h, g[2], w_up, layer)
    return ffn_down_ple(act, w_down, g[3], h, p, w_pp, w_pg, g[4])


def _even_prompt(h, b, s, g, w_in, w_out, gate_b, forget_b, pe, phi, slopes):
    z = prenorm_matmul(h, g[0], w_in).reshape(b, s, E_COLS)
    o_a, sel = nsa_compress_select(z, slopes, pe, phi, gate_b)
    o_a = nsa_selected(z, sel, o_a, slopes, gate_b)
    o_a = nsa_window(z, o_a, slopes, gate_b)
    logf, f_cum, f_cum_t = fox_gates(z, forget_b)
    fl = slice(E_FORGET_LANE, E_FORGET_LANE + FOX_HEADS)
    o_b = fox_attention(z, f_cum, f_cum_t[:, fl, :].reshape(b, FOX_HEADS, 1, s))
    m = b * s
    h = concat_matmul_postnorm_residual([o_a.reshape(m, -1), o_b.reshape(m, -1)], w_out, g[1], h)
    wb = min(NSA_WINDOW, s)
    win_off = E_KVA + 4 * NSA_KV * HEAD_DIM
    outs = (z[:, :, E_KVA:win_off].reshape(b, s, 4, NSA_KV, HEAD_DIM),
            z[:, :, E_KB:E_GATE].reshape(b, s, 2, FOX_HEADS, HEAD_DIM),
            logf[:, :, fl],
            z[:, s - wb:, win_off:E_QB].reshape(b, wb, 2, NSA_KV, HEAD_DIM))
    return h, outs


def _odd_prompt(h, b, s, g, w_in, w_out, sink, slopes):
    z = prenorm_matmul(h, g[0], w_in).reshape(b, s, O_COLS)
    o = swa_attention(z, slopes, sink)
    h = concat_matmul_postnorm_residual([o.reshape(b * s, -1)], w_out, g[1], h)
    wb = min(SWA_WINDOW, s)
    tail = z[:, s - wb:, O_K:].reshape(b, wb, 2, SWA_KV, 2, SWA_HEAD_DIM)[:, :, :, :, 0]
    return h, tail


NSA_POOL_PAGES = 16
FOX_PAGES = 16


def _nsa_pool_kernel(pt_ref, *refs, n_pages):
    page_refs, (pe_ref, o_ref) = refs[:n_pages], refs[n_pages:]
    blocks_per_page = PAGE_SIZE // NSA_BLOCK
    for i, ref in enumerate(page_refs):
        x = ref[...].reshape((blocks_per_page, NSA_BLOCK) + ref.shape[1:]) + pe_ref[...][None]
        o_ref[i * blocks_per_page:(i + 1) * blocks_per_page] = jnp.mean(x, axis=1)


def nsa_pool_pages(cache, page_table, pe_slabs):
    db, n_pages = page_table.shape
    slabs = cache.shape[2:]
    p = _pick(n_pages, NSA_POOL_PAGES)
    bpp = PAGE_SIZE // NSA_BLOCK

    def page_map(i):
        return lambda b, c, pt: (pt[b, c * p + i], 0, 0, 0)

    return pl.pallas_call(
        functools.partial(_nsa_pool_kernel, n_pages=p),
        out_shape=jax.ShapeDtypeStruct((db, n_pages * bpp) + slabs, F32),
        grid_spec=pltpu.PrefetchScalarGridSpec(
            num_scalar_prefetch=1,
            grid=(db, n_pages // p),
            in_specs=[pl.BlockSpec((PAGE_SIZE, None) + slabs, page_map(i)) for i in range(p)]
            + [pl.BlockSpec((NSA_BLOCK,) + slabs, lambda b, c, pt: (0, 0, 0))],
            out_specs=pl.BlockSpec((None, p * bpp) + slabs, lambda b, c, pt: (b, c, 0, 0))),
        compiler_params=_cparams(("parallel", "arbitrary")),
        name="nsa_pool_pages",
    )(page_table, *([cache] * p), pe_slabs)


def _transposed_rows(row):
    n = row.shape[1]
    bc = jnp.broadcast_to(row, (n, n))
    return bc.T, bc


def _nsa_sample_cmp_kernel(q_ref, pooled_ref, phi_ref, slopes_ref, gz_ref, gb_ref, o_ref, idx_ref,
                           *, q_pos, n_pick):
    n_slabs = 2 * NSA_KV
    npb = pooled_ref.shape[0] // n_slabs
    q = q_ref[...].astype(BF16)
    slab = lambda s: pooled_ref[pl.ds(s, npb, stride=n_slabs), :].astype(BF16)
    n_row = lax.broadcasted_iota(jnp.int32, (1, npb), 1)
    complete = (n_row + 1) * NSA_BLOCK - 1 <= q_pos
    dist = q_pos - (n_row.astype(F32) * NSA_BLOCK + (NSA_BLOCK - 1) / 2)
    head_row = lax.broadcasted_iota(jnp.int32, (NSA_HEADS, 1), 0)
    gates = jax.nn.sigmoid(gz_ref[...] + gb_ref[...])
    lane = lax.broadcasted_iota(jnp.int32, (NSA_HEADS, LANES), 1)
    gate_cmp = jnp.sum(jnp.where(lane == head_row, gates, 0.0), axis=-1, keepdims=True)
    scale = HEAD_DIM ** -0.5
    i_idx = lax.broadcasted_iota(jnp.int32, (npb, npb), 0)
    j_idx = lax.broadcasted_iota(jnp.int32, (npb, npb), 1)
    o_acc = jnp.zeros((NSA_HEADS, HEAD_DIM), F32)
    for g in range(NSA_KV):
        in_group = (head_row >= g * NSA_REP) & (head_row < (g + 1) * NSA_REP)
        kc = jnp.dot(slab(g), phi_ref[0].astype(BF16), preferred_element_type=F32).astype(BF16)
        vc = jnp.dot(slab(NSA_KV + g), phi_ref[1].astype(BF16), preferred_element_type=F32).astype(BF16)
        s = lax.dot_general(q, kc, (((1,), (1,)), ((), ())), preferred_element_type=F32) * scale
        s = s - slopes_ref[...] * dist
        s = jnp.where(complete, s, MASKED)
        m = jnp.max(s, axis=-1, keepdims=True)
        m = jnp.where(m > 0.5 * MASKED, m, 0.0)
        e = jnp.exp(s - m)
        p = e / jnp.maximum(jnp.sum(e, axis=-1, keepdims=True), 1e-30)
        o_g = jnp.dot(p.astype(BF16), vc, preferred_element_type=F32)
        o_acc = jnp.where(in_group, o_g, o_acc)
        imp = jnp.sum(jnp.where(in_group, p, 0.0), axis=0, keepdims=True)

        cur = q_pos // NSA_BLOCK
        forced = (n_row == 0) | (n_row == cur) | (n_row == cur - 1)
        score = jnp.where(forced, NSA_FORCE, jnp.where(complete, imp, -NSA_FORCE))
        col, bc = _transposed_rows(score)
        rank = jnp.sum(jnp.where((col > bc) | ((col == bc) & (i_idx < j_idx)), 1.0, 0.0),
                       axis=0, keepdims=True)
        sel = jnp.where(rank < n_pick, 1.0, 0.0)
        sel_col, _ = _transposed_rows(sel)
        pos = jnp.sum(jnp.where(i_idx < j_idx, sel_col, 0.0), axis=0, keepdims=True)
        slot = lax.broadcasted_iota(jnp.int32, (NSA_N_SEL, npb), 0).astype(F32)
        n_f = lax.broadcasted_iota(jnp.int32, (NSA_N_SEL, npb), 1).astype(F32)
        hit = (sel > 0.5) & (pos == slot)
        ids = jnp.sum(jnp.where(hit, n_f, 0.0), axis=-1, keepdims=True)
        idx_ref[g] = jnp.broadcast_to(ids, (NSA_N_SEL, LANES)).astype(jnp.int32)
    o_ref[...] = gate_cmp * o_acc


def nsa_sample_compress_select(q, pooled, phi, slopes, gz, gate_b, *, q_pos, n_pick):
    db = q.shape[0]
    pooled = pooled.reshape(db, -1, HEAD_DIM)
    npb = pooled.shape[1]
    kern = functools.partial(_nsa_sample_cmp_kernel, q_pos=q_pos, n_pick=n_pick)
    return pl.pallas_call(
        kern,
        out_shape=(jax.ShapeDtypeStruct((db, NSA_HEADS, HEAD_DIM), F32),
                   jax.ShapeDtypeStruct((db, NSA_KV, NSA_N_SEL, LANES), jnp.int32)),
        grid=(db,),
        in_specs=[pl.BlockSpec((None, NSA_HEADS, HEAD_DIM), lambda b: (b, 0, 0)),
                  pl.BlockSpec((None, npb, pooled.shape[2]), lambda b: (b, 0, 0)),
                  pl.BlockSpec(phi.shape, lambda b: (0, 0, 0)),
                  pl.BlockSpec((NSA_HEADS, 1), lambda b: (0, 0)),
                  pl.BlockSpec((None, 1, LANES), lambda b: (b, 0, 0)),
                  pl.BlockSpec((1, LANES), lambda b: (0, 0))],
        out_specs=(pl.BlockSpec((None, NSA_HEADS, HEAD_DIM), lambda b: (b, 0, 0)),
                   pl.BlockSpec((None, NSA_KV, NSA_N_SEL, LANES), lambda b: (b, 0, 0, 0))),
        compiler_params=_cparams(("parallel",)),
        name="nsa_sample_compress_select",
    )(q, pooled, phi, slopes.reshape(NSA_HEADS, 1), gz, gate_b)


def _nsa_sample_attend_kernel(idx_ref, pt_ref, *refs, n_pick, q_pos):
    n_blk = NSA_KV * n_pick
    n_slabs = 4 * NSA_KV
    blk_refs = refs[:n_blk]
    q_ref, new_ref, win_ref, prev_ref, slopes_ref, gz_ref, gb_ref, o_ref, win_out_ref = refs[n_blk:]
    b = pl.program_id(0)
    n_win_slabs = 2 * NSA_KV
    w = win_ref.shape[0] // n_win_slabs
    scale = HEAD_DIM ** -0.5
    gates = jax.nn.sigmoid(gz_ref[...] + gb_ref[...])
    r_col = lax.broadcasted_iota(jnp.int32, (NSA_REP, 1), 0)
    lane = lax.broadcasted_iota(jnp.int32, (NSA_REP, LANES), 1)
    row_w = lax.broadcasted_iota(jnp.int32, (w, HEAD_DIM), 0)
    n_keys = n_pick * NSA_BLOCK
    key_lane = lax.broadcasted_iota(jnp.int32, (1, n_keys), 1)

    def rounded(x):
        return x.astype(BF16).astype(F32)

    def attend(q4, keys, vals, k_new, v_new, dist, valid, slope):
        s = lax.dot_general(q4.astype(BF16), keys.astype(BF16), (((1,), (1,)), ((), ())),
                            preferred_element_type=F32) * scale - slope * dist
        s = jnp.where(valid, s, MASKED)
        s_new = jnp.sum(rounded(q4) * rounded(k_new), axis=-1, keepdims=True) * scale
        m = jnp.maximum(jnp.max(s, axis=-1, keepdims=True), s_new)
        e = jnp.exp(s - m)
        e_new = jnp.exp(s_new - m)
        den = jnp.sum(e, axis=-1, keepdims=True) + e_new
        num = jnp.dot(e.astype(BF16), vals.astype(BF16), preferred_element_type=F32)
        return (num + rounded(e_new) * rounded(v_new)) / den

    for g in range(NSA_KV):
        q4 = q_ref[g]
        slope = slopes_ref[g]
        new = lambda c: new_ref[c * NSA_KV + g:c * NSA_KV + g + 1, :]
        gate = lambda c: jnp.sum(jnp.where(lane == c * NSA_HEADS + g * NSA_REP + r_col, gates, 0.0),
                                 axis=-1, keepdims=True)
        blocks = blk_refs[g * n_pick:(g + 1) * n_pick]
        slab = lambda r, c: r[pl.ds(c * NSA_KV + g, NSA_BLOCK, stride=n_slabs), :]
        keys = jnp.concatenate([slab(r, 2) for r in blocks], axis=0)
        vals = jnp.concatenate([slab(r, 3) for r in blocks], axis=0)
        k_pos = key_lane & (NSA_BLOCK - 1)
        for i in range(n_pick):
            in_blk = jnp.right_shift(key_lane, NSA_BLOCK_SHIFT) == i
            k_pos = k_pos + jnp.where(in_blk, idx_ref[b, g, i] * NSA_BLOCK, 0)
        dist = q_pos - k_pos
        o_sel = attend(q4, keys, vals, new(2), new(3), dist.astype(F32), dist >= 0, slope)
        win_slab = lambda c: pl.ds(c * NSA_KV + g, w, stride=n_win_slabs)
        k_w = win_ref[win_slab(0), :]
        v_w = win_ref[win_slab(1), :]
        dist_w = w - lax.broadcasted_iota(jnp.int32, (1, w), 1)
        o_win = attend(q4, k_w, v_w, new(4), new(5), dist_w.astype(F32), dist_w <= NSA_WINDOW, slope)
        o_ref[g] = prev_ref[g] + gate(1) * o_sel + gate(2) * o_win
        for c, buf in ((0, k_w), (1, v_w)):
            rolled = pltpu.roll(buf, shift=w - 1, axis=0)
            win_out_ref[win_slab(c), :] = jnp.where(row_w == w - 1, new(4 + c), rolled)


def nsa_sample_attend(cache, idx, page_table, q, new_rows, win, prev, slopes, gz, gate_b, *, n_pick, q_pos):
    db = q.shape[0]
    bpp = PAGE_SIZE // NSA_BLOCK
    bpp_shift = bpp.bit_length() - 1

    def blk_map(g, i):
        def index(b, idx_r, pt_r):
            blk = idx_r[b, g, i]
            return pt_r[b, jnp.right_shift(blk, bpp_shift)] * bpp + (blk & (bpp - 1)), 0
        return index

    blk_specs = [pl.BlockSpec((NSA_BLOCK * 4 * NSA_KV, HEAD_DIM), blk_map(g, i))
                 for g in range(NSA_KV) for i in range(n_pick)]
    full = lambda shape: pl.BlockSpec(shape, lambda b, idx_r, pt_r: (0,) * len(shape))
    per_b = lambda shape: pl.BlockSpec((None,) + shape, lambda b, idx_r, pt_r: (b,) + (0,) * len(shape))
    kern = functools.partial(_nsa_sample_attend_kernel, n_pick=n_pick, q_pos=q_pos)
    return pl.pallas_call(
        kern,
        out_shape=(jax.ShapeDtypeStruct((db, NSA_KV, NSA_REP, HEAD_DIM), F32),
                   jax.ShapeDtypeStruct(win.shape, F32)),
        grid_spec=pltpu.PrefetchScalarGridSpec(
            num_scalar_prefetch=2,
            grid=(db,),
            in_specs=blk_specs + [per_b((NSA_KV, NSA_REP, HEAD_DIM)), per_b(new_rows.shape[1:]),
                                  per_b(win.shape[1:]), per_b((NSA_KV, NSA_REP, HEAD_DIM)),
                                  full((NSA_KV, NSA_REP, 1)), per_b((1, LANES)), full((1, LANES))],
            out_specs=(per_b((NSA_KV, NSA_REP, HEAD_DIM)), per_b(win.shape[1:]))),
        compiler_params=_cparams(("arbitrary",)),
        name="nsa_sample_attend",
    )(idx, page_table, *([cache] * len(blk_specs)), q, new_rows, win, prev,
      slopes.reshape(NSA_KV, NSA_REP, 1), gz, gate_b)


def _fox_decode_kernel(pt_ref, *refs, n_pages):
    kv_refs, lf_refs = refs[:n_pages], refs[n_pages:2 * n_pages]
    (q_ref, knew_ref, vnew_ref, vrow_ref, fraw_ref, fb_ref,
     o_ref, lfnew_ref, qbd_ref, m_ref, l_ref, acc_ref, carry_ref) = refs[2 * n_pages:]
    c = pl.program_id(1)
    width = FOX_HEADS * HEAD_DIM
    scale = HEAD_DIM ** -0.5
    head_row = lax.broadcasted_iota(jnp.int32, (FOX_HEADS, width), 0)
    head_of_lane = jnp.right_shift(lax.broadcasted_iota(jnp.int32, (FOX_HEADS, width), 1),
                                   HEAD_DIM.bit_length() - 1)

    @pl.when(c == 0)
    def _():
        q = q_ref[...]
        qbd_ref[...] = jnp.where(head_of_lane == head_row, jnp.concatenate([q] * FOX_HEADS, axis=1),
                                 0.0).astype(BF16)
        x = fraw_ref[...] + fb_ref[...]
        lf_new = jnp.minimum(x, 0.0) - jnp.log1p(jnp.exp(-jnp.abs(x)))
        lfnew_ref[...] = lf_new
        carry_ref[...] = lf_new
        rounded = lambda t: t.astype(BF16).astype(F32)
        m_ref[...] = jnp.sum(rounded(q) * rounded(knew_ref[...]), axis=-1, keepdims=True) * scale
        l_ref[...] = jnp.ones_like(l_ref)
        acc_ref[...] = jnp.broadcast_to(rounded(vrow_ref[...]), acc_ref.shape)

    later = jnp.where(lax.broadcasted_iota(jnp.int32, (PAGE_SIZE, PAGE_SIZE), 0)
                      > lax.broadcasted_iota(jnp.int32, (PAGE_SIZE, PAGE_SIZE), 1), 1.0, 0.0).astype(BF16)
    lf_all = jnp.concatenate([lf_ref[...] for lf_ref in lf_refs], axis=0)
    rows = lf_all.shape[0]
    suffix3 = jnp.dot(jnp.concatenate(_split3(lf_all), axis=0), later, preferred_element_type=F32)
    suffix = suffix3[:rows] + suffix3[rows:2 * rows] + suffix3[2 * rows:]
    totals = jnp.sum(lf_all, axis=-1, keepdims=True)
    carry = carry_ref[...]
    biases = []
    for i in range(n_pages):
        page_rows = slice(i * FOX_HEADS, (i + 1) * FOX_HEADS)
        biases.append(carry + suffix[page_rows])
        carry = carry + totals[page_rows]
    carry_ref[...] = carry
    bias = jnp.concatenate(biases, axis=1)
    def heads_on_lanes(kv_ref, first):
        return jnp.concatenate([kv_ref[pl.ds(first + h, PAGE_SIZE, stride=2 * FOX_HEADS), :].astype(BF16)
                                for h in range(FOX_HEADS)], axis=1)

    kp = jnp.concatenate([heads_on_lanes(kv_ref, 0) for kv_ref in kv_refs], axis=0)
    vp = jnp.concatenate([heads_on_lanes(kv_ref, FOX_HEADS) for kv_ref in kv_refs], axis=0)
    s = lax.dot_general(qbd_ref[...], kp, (((1,), (1,)), ((), ())),
                        preferred_element_type=F32) * scale + bias
    m_prev = m_ref[...]
    m_new = jnp.maximum(m_prev, jnp.max(s, axis=-1, keepdims=True))
    alpha = jnp.exp(m_prev - m_new)
    p = jnp.exp(s - m_new)
    l_ref[...] = alpha * l_ref[...] + jnp.sum(p, axis=-1, keepdims=True)
    acc_ref[...] = alpha * acc_ref[...] + jnp.dot(p.astype(BF16), vp, preferred_element_type=F32)
    m_ref[...] = m_new

    @pl.when(c == pl.num_programs(1) - 1)
    def _():
        acc = jnp.where(head_of_lane == head_row, acc_ref[...], 0.0)
        o = acc[:, :HEAD_DIM]
        for h in range(1, FOX_HEADS):
            o = o + acc[:, h * HEAD_DIM:(h + 1) * HEAD_DIM]
        o_ref[...] = o / l_ref[...]


def fox_decode(cache_kv, cache_logf_t, page_table, q, k_new, v_new, f_raw, forget_b):
    db, n_pages = page_table.shape
    p = _pick(n_pages, FOX_PAGES)
    width = FOX_HEADS * HEAD_DIM
    page_rows = PAGE_SIZE * 2 * FOX_HEADS

    def page_map(i):
        return lambda b, c, pt: (pt[b, n_pages - 1 - (c * p + i)], 0, 0)

    def kv_page_map(i):
        return lambda b, c, pt: (pt[b, n_pages - 1 - (c * p + i)], 0)

    per_b = lambda shape: pl.BlockSpec((None,) + shape, lambda b, c, pt: (b,) + (0,) * len(shape))
    return pl.pallas_call(
        functools.partial(_fox_decode_kernel, n_pages=p),
        out_shape=(jax.ShapeDtypeStruct((db, FOX_HEADS, HEAD_DIM), F32),
                   jax.ShapeDtypeStruct((db, FOX_HEADS, 1), F32)),
        grid_spec=pltpu.PrefetchScalarGridSpec(
            num_scalar_prefetch=1,
            grid=(db, n_pages // p),
            in_specs=[pl.BlockSpec((page_rows, HEAD_DIM), kv_page_map(i)) for i in range(p)]
            + [pl.BlockSpec((None, FOX_HEADS, PAGE_SIZE), page_map(i)) for i in range(p)]
            + [per_b((FOX_HEADS, HEAD_DIM)), per_b((FOX_HEADS, HEAD_DIM)), per_b((FOX_HEADS, HEAD_DIM)),
               per_b((1, width)), per_b((FOX_HEADS, 1)),
               pl.BlockSpec((FOX_HEADS, 1), lambda b, c, pt: (0, 0))],
            out_specs=(per_b((FOX_HEADS, HEAD_DIM)), per_b((FOX_HEADS, 1))),
            scratch_shapes=[pltpu.VMEM((FOX_HEADS, width), BF16), pltpu.VMEM((FOX_HEADS, 1), F32),
                            pltpu.VMEM((FOX_HEADS, 1), F32), pltpu.VMEM((FOX_HEADS, width), F32),
                            pltpu.VMEM((FOX_HEADS, 1), F32)]),
        compiler_params=_cparams(("parallel", "arbitrary")),
        name="fox_decode",
    )(page_table, *([cache_kv] * p), *([cache_logf_t] * p), q, k_new, v_new,
      v_new.reshape(db, 1, width), f_raw, forget_b.reshape(FOX_HEADS, 1))


def _swa_decode_kernel(q_ref, new_ref, win_ref, slopes_ref, sink_ref, o_ref, win_out_ref):
    w = win_ref.shape[0]
    kvw = SWA_KV * SWA_HEAD_DIM
    scale = SWA_HEAD_DIM ** -0.5
    rounded = lambda t: t.astype(BF16).astype(F32)
    dist = (w - lax.broadcasted_iota(jnp.int32, (1, w), 1))
    new = new_ref[...]
    for g in range(SWA_KV):
        q = q_ref[g]
        lo = g * SWA_HEAD_DIM
        k_w = win_ref[:, lo:lo + SWA_HEAD_DIM]
        v_w = win_ref[:, kvw + lo:kvw + lo + SWA_HEAD_DIM]
        k_new = new[:, lo:lo + SWA_HEAD_DIM]
        v_new = new[:, kvw + lo:kvw + lo + SWA_HEAD_DIM]
        s = lax.dot_general(q.astype(BF16), k_w.astype(BF16), (((1,), (1,)), ((), ())),
                            preferred_element_type=F32) * scale - slopes_ref[g] * dist.astype(F32)
        s = jnp.where(dist <= SWA_WINDOW, s, MASKED)
        s_new = jnp.sum(rounded(q) * rounded(k_new), axis=-1, keepdims=True) * scale
        sink = sink_ref[g]
        m = jnp.maximum(jnp.maximum(jnp.max(s, axis=-1, keepdims=True), s_new), sink)
        e = jnp.exp(s - m)
        e_new = jnp.exp(s_new - m)
        den = jnp.sum(e, axis=-1, keepdims=True) + e_new + jnp.exp(sink - m)
        num = jnp.dot(e.astype(BF16), v_w.astype(BF16), preferred_element_type=F32)
        o_ref[g] = (num + rounded(e_new) * rounded(v_new)) / den
    buf = win_ref[...]
    row = lax.broadcasted_iota(jnp.int32, buf.shape, 0)
    win_out_ref[...] = jnp.where(row == w - 1, new, pltpu.roll(buf, shift=w - 1, axis=0))


def swa_decode(q, new_row, win, slopes, sink):
    db = q.shape[0]
    per_b = lambda shape: pl.BlockSpec((None,) + shape, lambda b: (b,) + (0,) * len(shape))
    full = lambda shape: pl.BlockSpec(shape, lambda b: (0,) * len(shape))
    return pl.pallas_call(
        _swa_decode_kernel,
        out_shape=(jax.ShapeDtypeStruct(q.shape, F32), jax.ShapeDtypeStruct(win.shape, F32)),
        grid=(db,),
        in_specs=[per_b(q.shape[1:]), per_b(new_row.shape[1:]), per_b(win.shape[1:]),
                  full((SWA_KV, SWA_REP, 1)), full((SWA_KV, SWA_REP, 1))],
        out_specs=(per_b(q.shape[1:]), per_b(win.shape[1:])),
        compiler_params=_cparams(("parallel",)),
        name="swa_decode",
    )(q, new_row, win, slopes.reshape(SWA_KV, SWA_REP, 1), sink.reshape(SWA_KV, SWA_REP, 1))


def _even_sample(h, g, w_in, w_out, gate_b, forget_b, pe, phi, slopes,
                 cache_nsa, cache_fox_kv, cache_fox_logf, win_buf, page_table, e):
    db = h.shape[0]
    past = page_table.shape[1] * PAGE_SIZE
    n_pick = NSA_N_SEL - 1
    assert past % NSA_BLOCK == 0 and past // NSA_BLOCK >= n_pick
    n_even, pool = cache_nsa.shape[:2]
    page_table = page_table + e * pool
    nsa_rows = cache_nsa.reshape(-1, HEAD_DIM)
    nsa_halves = cache_nsa.reshape(-1, 2, 2 * NSA_KV, HEAD_DIM)
    fox_rows = cache_fox_kv.reshape(-1, HEAD_DIM)
    logf_t = cache_fox_logf.reshape((n_even * pool,) + cache_fox_logf.shape[2:]).transpose(0, 2, 1)
    win_buf = win_buf.reshape((n_even * db,) + win_buf.shape[2:])[e * db:(e + 1) * db]
    z = prenorm_matmul(h, g[0], w_in)
    heads = lambda lo, hi: z[:, lo:hi].reshape(db, -1, HEAD_DIM)
    gz = z[:, E_GATE:E_GATE + LANES].reshape(db, 1, LANES)
    q_a = heads(E_QA, E_KVA)
    pe_slabs = jnp.stack([pe[0]] * NSA_KV + [pe[1]] * NSA_KV, axis=1)
    pooled = nsa_pool_pages(nsa_halves, page_table, pe_slabs)
    o_cmp, idx = nsa_sample_compress_select(q_a, pooled, phi, slopes, gz, gate_b, q_pos=past, n_pick=n_pick)
    grp = lambda x: x.reshape(db, NSA_KV, NSA_REP, HEAD_DIM)
    o_a, new_win = nsa_sample_attend(nsa_rows, idx[..., 0], page_table, grp(q_a), heads(E_KVA, E_QB),
                                     win_buf.reshape(db, -1, HEAD_DIM), grp(o_cmp), slopes, gz, gate_b,
                                     n_pick=n_pick, q_pos=past)
    o_b, lf_new = fox_decode(fox_rows, logf_t,
                             page_table, heads(E_QB, E_KB), heads(E_KB, E_VB), heads(E_VB, E_GATE),
                             z[:, E_GATE + E_FORGET_LANE:E_GATE + E_FORGET_LANE + FOX_HEADS].reshape(db, -1, 1),
                             forget_b)
    h = concat_matmul_postnorm_residual([o_a.reshape(db, -1), o_b.reshape(db, -1)], w_out, g[1], h)
    win_off = E_KVA + 4 * NSA_KV * HEAD_DIM
    outs = (z[:, E_KVA:win_off].reshape(db, 1, 4, NSA_KV, HEAD_DIM),
            z[:, E_KB:E_GATE].reshape(db, 1, 2, FOX_HEADS, HEAD_DIM),
            lf_new.reshape(db, 1, FOX_HEADS),
            new_win.reshape(win_buf.shape))
    return h, outs


def _odd_sample(h, g, w_in, w_out, sink, slopes, win_buf):
    db = h.shape[0]
    w = win_buf.shape[1]
    z = prenorm_matmul(h, g[0], w_in)
    q = z[:, :O_K].reshape(db, SWA_KV, SWA_REP, SWA_HEAD_DIM)
    new_row = z[:, O_K:].reshape(db, 2, SWA_KV, 2, SWA_HEAD_DIM)[:, :, :, 0].reshape(db, 1, -1)
    o, new_win = swa_decode(q, new_row, win_buf.reshape(db, w, -1), slopes, sink)
    h = concat_matmul_postnorm_residual([o.reshape(db, -1)], w_out, g[1], h)
    return h, new_win.reshape(win_buf.shape)


def kernel(x_prompt, x_sample, cache_nsa_kv, cache_fox_kv, cache_fox_logf, state_nsa_win, state_swa_win,
           page_table, p_prompt, p_sample, w_in_even, w_out_even, nsa_gate_b, nsa_pe, nsa_phi, fox_forget_b,
           w_in_odd, w_out_odd, swa_sink, norm_g, w_ffn_up, w_ffn_down, w_ple_proj, w_ple_gate):
    b, s, d = x_prompt.shape
    db, dec_seq, _ = x_sample.shape
    assert dec_seq == 1, "the decode kernels handle one new token per sequence"
    depth = norm_g.shape[0]
    nsa_slopes = _alibi_slopes(NSA_HEADS)
    swa_slopes = _alibi_slopes(SWA_HEADS)
    hp = x_prompt.reshape(b * s, d)
    hs = x_sample.reshape(db, d)
    even_p, even_s, odd_p, odd_s = [], [], [], []
    for i in range(depth):
        g = norm_g[i]
        if i % 2 == 0:
            e = i // 2
            w_in = _prep_even_w_in(w_in_even[e])
            w_out = w_out_even[e].astype(BF16)
            gate_b = _lane_row(nsa_gate_b[e], 0)
            hp, outs = _even_prompt(hp, b, s, g, w_in, w_out, gate_b,
                                    _lane_row(fox_forget_b[e], E_FORGET_LANE), nsa_pe[e], nsa_phi[e], nsa_slopes)
            even_p.append(outs)
            hs, outs = _even_sample(hs, g, w_in, w_out, gate_b, fox_forget_b[e], nsa_pe[e], nsa_phi[e],
                                    nsa_slopes, cache_nsa_kv, cache_fox_kv, cache_fox_logf,
                                    state_nsa_win, page_table, e)
            even_s.append(outs)
        else:
            o = i // 2
            w_in = _prep_odd_w_in(w_in_odd[o])
            w_out = w_out_odd[o].astype(BF16)
            hp, tail = _odd_prompt(hp, b, s, g, w_in, w_out, swa_sink[o], swa_slopes)
            odd_p.append(tail)
            hs, new_win = _odd_sample(hs, g, w_in, w_out, swa_sink[o], swa_slopes, state_swa_win[o])
            odd_s.append(new_win)
        ffn_w = (w_ffn_up, w_ffn_down[i].astype(BF16),
                 w_ple_proj[i].astype(BF16), w_ple_gate[i].astype(BF16))
        hp = _ffn_and_ple(hp, p_prompt[i].reshape(b * s, -1), g, i, *ffn_w)
        hs = _ffn_and_ple(hs, p_sample[i].reshape(db, -1), g, i, *ffn_w)
    stack = lambda seq, k: jnp.stack([t[k] for t in seq])
    return (hp.reshape(b, s, d), hs.reshape(db, 1, d),
            stack(even_p, 0), stack(even_s, 0), stack(even_p, 1), stack(even_s, 1),
            stack(even_p, 2), stack(even_s, 2), stack(even_p, 3), stack(even_s, 3),
            jnp.stack(odd_p), jnp.stack(odd_s))
```

```python
import functools

import jax
import jax.numpy as jnp
from jax import lax
from jax.experimental import pallas as pl
from jax.experimental.pallas import tpu as pltpu

F32 = jnp.float32
BF16 = jnp.bfloat16

HEAD_DIM = 128
NSA_HEADS = 8
NSA_KV = 2
NSA_REP = NSA_HEADS // NSA_KV
NSA_BLOCK = 64
NSA_BLOCK_SHIFT = 6
NSA_N_SEL = 16
NSA_WINDOW = 512
NSA_FORCE = 1e4
FOX_HEADS = 8
SWA_HEADS = 32
SWA_KV = 4
SWA_REP = SWA_HEADS // SWA_KV
SWA_HEAD_DIM = 64
SWA_WINDOW = 128
PAGE_SIZE = 128
EPS = 1e-6

LANES = 128
MASKED = -1e30
LOG2E = 1.4426950408889634
LOGIT_CHUNK_VREGS = 256
VMEM_LIMIT = 56 * 1024 * 1024

E_QA = 0
E_KVA = NSA_HEADS * HEAD_DIM
E_QB = E_KVA + 6 * NSA_KV * HEAD_DIM
E_KB = E_QB + FOX_HEADS * HEAD_DIM
E_VB = E_KB + FOX_HEADS * HEAD_DIM
E_GATE = E_VB + FOX_HEADS * HEAD_DIM
E_FORGET_LANE = 3 * NSA_HEADS
E_COLS = 6144
O_Q = 0
O_K = SWA_HEADS * SWA_HEAD_DIM
O_V = O_K + SWA_KV * LANES
O_COLS = O_V + SWA_KV * LANES


def _cparams(sem):
    return pltpu.CompilerParams(dimension_semantics=sem, vmem_limit_bytes=VMEM_LIMIT)


def _pick(n, pref):
    t = min(n, pref)
    while n % t:
        t //= 2
    return t


def _rms(x, g):
    return x * lax.rsqrt(jnp.mean(x * x, axis=-1, keepdims=True) + EPS) * g


def _split3(x):
    hi = x.astype(BF16)
    r = x - hi.astype(F32)
    mid = r.astype(BF16)
    lo = (r - mid.astype(F32)).astype(BF16)
    return hi, mid, lo


def _prenorm_mm_kernel(x_ref, g_ref, w_ref, o_ref, xn_ref):
    @pl.when(pl.program_id(1) == 0)
    def _():
        xn_ref[...] = _rms(x_ref[...], g_ref[...]).astype(BF16)

    o_ref[...] = jnp.dot(xn_ref[...], w_ref[...], preferred_element_type=F32)


def prenorm_matmul(x, g, w, *, tm=512, tn=1024):
    m, k = x.shape
    n = w.shape[1]
    tm, tn = _pick(m, tm), _pick(n, tn)
    return pl.pallas_call(
        _prenorm_mm_kernel,
        out_shape=jax.ShapeDtypeStruct((m, n), F32),
        grid=(m // tm, n // tn),
        in_specs=[pl.BlockSpec((tm, k), lambda i, j: (i, 0)),
                  pl.BlockSpec((1, k), lambda i, j: (0, 0)),
                  pl.BlockSpec((k, tn), lambda i, j: (0, j))],
        out_specs=pl.BlockSpec((tm, tn), lambda i, j: (i, j)),
        scratch_shapes=[pltpu.VMEM((tm, k), BF16)],
        compiler_params=_cparams(("parallel", "arbitrary")),
        name="prenorm_matmul",
    )(x, g.reshape(1, k), w)


def _prenorm_swiglu_kernel(x_ref, g_ref, wg_ref, wu_ref, o_ref, xn_ref):
    @pl.when(pl.program_id(1) == 0)
    def _():
        xn_ref[...] = _rms(x_ref[...], g_ref[...]).astype(BF16)

    xn = xn_ref[...]
    gate = jnp.dot(xn, wg_ref[...].astype(BF16), preferred_element_type=F32)
    up = jnp.dot(xn, wu_ref[...].astype(BF16), preferred_element_type=F32)
    o_ref[...] = (gate * jax.nn.sigmoid(gate) * up).astype(BF16)


def prenorm_swiglu(x, g, w_up, layer, *, tm=1024, tn=512):
    m, k = x.shape
    f = w_up.shape[2] // 2
    tm, tn = _pick(m, tm), _pick(f, tn)
    nj = f // tn
    return pl.pallas_call(
        _prenorm_swiglu_kernel,
        out_shape=jax.ShapeDtypeStruct((m, f), BF16),
        grid=(m // tm, nj),
        in_specs=[pl.BlockSpec((tm, k), lambda i, j: (i, 0)),
                  pl.BlockSpec((1, k), lambda i, j: (0, 0)),
                  pl.BlockSpec((None, k, tn), lambda i, j: (layer, 0, j)),
                  pl.BlockSpec((None, k, tn), lambda i, j: (layer, 0, j + nj))],
        out_specs=pl.BlockSpec((tm, tn), lambda i, j: (i, j)),
        scratch_shapes=[pltpu.VMEM((tm, k), BF16)],
        compiler_params=_cparams(("parallel", "arbitrary")),
        name="prenorm_swiglu",
    )(x, g.reshape(1, k), w_up, w_up)


def _mmn_postnorm_res_kernel(*refs, k_sizes):
    n_in = len(k_sizes)
    a_refs, (w_ref, g_ref, h_ref, o_ref) = refs[:n_in], refs[n_in:]
    acc = None
    off = 0
    for a_ref, ks in zip(a_refs, k_sizes):
        part = jnp.dot(a_ref[...].astype(BF16), w_ref[off:off + ks, :], preferred_element_type=F32)
        acc = part if acc is None else acc + part
        off += ks
    o_ref[...] = h_ref[...] + _rms(acc, g_ref[...])


def concat_matmul_postnorm_residual(a_list, w, g, h, *, tm=256):
    m, n = h.shape
    k_sizes = tuple(a.shape[1] for a in a_list)
    tm = _pick(m, tm)
    return pl.pallas_call(
        functools.partial(_mmn_postnorm_res_kernel, k_sizes=k_sizes),
        out_shape=jax.ShapeDtypeStruct((m, n), F32),
        grid=(m // tm,),
        in_specs=[pl.BlockSpec((tm, ks), lambda i: (i, 0)) for ks in k_sizes]
        + [pl.BlockSpec(w.shape, lambda i: (0, 0)),
           pl.BlockSpec((1, n), lambda i: (0, 0)),
           pl.BlockSpec((tm, n), lambda i: (i, 0))],
        out_specs=pl.BlockSpec((tm, n), lambda i: (i, 0)),
        compiler_params=_cparams(("parallel",)),
        name="concat_matmul_postnorm_residual",
    )(*a_list, w, g.reshape(1, n), h)


def _lane_pick(x, lane):
    idx = lax.broadcasted_iota(jnp.int32, x.shape, x.ndim - 1)
    return jnp.sum(jnp.where(idx == lane, x, 0.0), axis=-1, keepdims=True)


def _nsa_cmp_kernel(slopes_ref, q_ref, ksrc_ref, vsrc_ref, pe_ref, phi_ref, gz_ref, gb_ref,
                    o_ref, sel_ref, kc_ref, vc_ref, *, tq, nb, nbp):
    g = pl.program_id(1)
    qi = pl.program_id(2)

    @pl.when(qi == 0)
    def _():
        for src, dst, c in ((ksrc_ref, kc_ref, 0), (vsrc_ref, vc_ref, 1)):
            x = src[...].reshape(nb, NSA_BLOCK, HEAD_DIM) + pe_ref[c][None]
            pooled = jnp.mean(x, axis=1)
            dst[...] = jnp.zeros_like(dst)
            dst[0:nb, :] = jnp.dot(pooled.astype(BF16), phi_ref[c].astype(BF16),
                                   preferred_element_type=F32)

    q = q_ref[...]
    q_pos = qi * tq + lax.broadcasted_iota(jnp.int32, (tq, nbp), 0)
    n_idx = lax.broadcasted_iota(jnp.int32, (tq, nbp), 1)
    complete = (n_idx + 1) * NSA_BLOCK - 1 <= q_pos
    valid = complete & (n_idx < nb)
    dist = q_pos.astype(F32) - (n_idx.astype(F32) * NSA_BLOCK + (NSA_BLOCK - 1) / 2)
    kcb = kc_ref[...].astype(BF16)
    vcb = vc_ref[...].astype(BF16)
    gates = jax.nn.sigmoid(gz_ref[...] + gb_ref[...])
    scale = HEAD_DIM ** -0.5
    imp = jnp.zeros((tq, nbp), F32)
    for r in range(NSA_REP):
        qr = q[:, r * HEAD_DIM:(r + 1) * HEAD_DIM].astype(BF16)
        s = lax.dot_general(qr, kcb, (((1,), (1,)), ((), ())), preferred_element_type=F32) * scale
        s = s - slopes_ref[g * NSA_REP + r] * dist
        s = jnp.where(valid, s, MASKED)
        m = jnp.max(s, axis=-1, keepdims=True)
        m = jnp.where(m > 0.5 * MASKED, m, 0.0)
        e = jnp.exp(s - m)
        p = e / jnp.maximum(jnp.sum(e, axis=-1, keepdims=True), 1e-30)
        imp = imp + p
        o_r = jnp.dot(p.astype(BF16), vcb, preferred_element_type=F32)
        o_ref[:, r * HEAD_DIM:(r + 1) * HEAD_DIM] = _lane_pick(gates, g * NSA_REP + r) * o_r

    cur = jnp.right_shift(q_pos, NSA_BLOCK_SHIFT)
    forced = (n_idx == 0) | (n_idx == cur) | (n_idx == cur - 1)
    score = jnp.where(forced, NSA_FORCE, jnp.where(complete, imp, -NSA_FORCE))
    score = jnp.where(n_idx < nb, score, -jnp.inf)
    rank = jnp.zeros((tq, nbp), F32)
    for i in range(nb):
        ci = score[:, i:i + 1]
        rank = rank + jnp.where((ci > score) | ((ci == score) & (n_idx > i)), 1.0, 0.0)
    sel_ref[...] = jnp.where((rank < min(NSA_N_SEL, nb)) & (n_idx < nb), 1.0, 0.0)


def nsa_compress_select(z, slopes, pe, phi, gate_b, *, tq=256):
    b, s, _ = z.shape
    nb = s // NSA_BLOCK
    assert s % NSA_BLOCK == 0 and nb % 8 == 0 and nb <= LANES
    tq = _pick(s, tq)
    cb = lambda off: off // HEAD_DIM
    kern = functools.partial(_nsa_cmp_kernel, tq=tq, nb=nb, nbp=LANES)
    return pl.pallas_call(
        kern,
        out_shape=(jax.ShapeDtypeStruct((b, s, NSA_HEADS * HEAD_DIM), F32),
                   jax.ShapeDtypeStruct((b, NSA_KV, s, LANES), F32)),
        grid=(b, NSA_KV, s // tq),
        in_specs=[pl.BlockSpec(memory_space=pltpu.SMEM),
                  pl.BlockSpec((None, tq, NSA_REP * HEAD_DIM), lambda bi, g, i: (bi, i, g)),
                  pl.BlockSpec((None, s, HEAD_DIM), lambda bi, g, i: (bi, 0, cb(E_KVA) + g)),
                  pl.BlockSpec((None, s, HEAD_DIM), lambda bi, g, i: (bi, 0, cb(E_KVA) + NSA_KV + g)),
                  pl.BlockSpec(pe.shape, lambda bi, g, i: (0, 0, 0)),
                  pl.BlockSpec(phi.shape, lambda bi, g, i: (0, 0, 0)),
                  pl.BlockSpec((None, tq, LANES), lambda bi, g, i: (bi, i, cb(E_GATE))),
                  pl.BlockSpec((1, LANES), lambda bi, g, i: (0, 0))],
        out_specs=(pl.BlockSpec((None, tq, NSA_REP * HEAD_DIM), lambda bi, g, i: (bi, i, g)),
                   pl.BlockSpec((None, None, tq, LANES), lambda bi, g, i: (bi, g, i, 0))),
        scratch_shapes=[pltpu.VMEM((LANES, HEAD_DIM), F32), pltpu.VMEM((LANES, HEAD_DIM), F32)],
        compiler_params=_cparams(("parallel", "parallel", "arbitrary")),
        name="nsa_compress_select",
    )(slopes, z, z, z, pe, phi, z, gate_b)


def _kv_tile(qi, j, *, tq, tk, window):
    q0 = qi * tq
    first = 0 if window is None else lax.div(jnp.maximum(q0 - window, 0), tk)
    last = lax.div(q0 + tq - 1, tk)
    return first + j, last


def _flash_t_kernel(*refs, mode, tq, tk, rep, nsteps, window, branch, rc):
    it = iter(refs)
    slopes_ref = next(it) if mode in ("sel", "win", "swa") else None
    sink_ref = next(it) if mode == "swa" else None
    q_ref, k_ref, v_ref = next(it), next(it), next(it)
    sel_ref = next(it) if mode == "sel" else None
    fq_ref, fk_ref = (next(it), next(it)) if mode == "fox" else (None, None)
    prev_ref, gz_ref, gb_ref = (next(it), next(it), next(it)) if mode in ("sel", "win") else (None,) * 3
    o_ref, m_ref, l_ref, acc_ref = next(it), next(it), next(it), next(it)

    g = pl.program_id(1)
    qi = pl.program_id(2)
    j = pl.program_id(3)
    kj, kv_last = _kv_tile(qi, j, tq=tq, tk=tk, window=window)
    head_dim = SWA_HEAD_DIM if mode == "swa" else HEAD_DIM
    qscale = head_dim ** -0.5 * LOG2E
    q0 = qi * tq
    n_chunks = tq // rc

    head_lanes = lambda r: slice(r * rc, (r + 1) * rc)

    @pl.when(j == 0)
    def _():
        if mode == "swa":
            for c in range(n_chunks):
                t_rel = (c * rc + lax.broadcasted_iota(jnp.int32, (1, rc), 1)).astype(F32)
                for r in range(rep):
                    m_ref[c, :, head_lanes(r)] = (sink_ref[g * rep + r]
                                                  + slopes_ref[g * rep + r] * t_rel) * LOG2E
            l_ref[...] = jnp.ones_like(l_ref)
        else:
            m_ref[...] = jnp.full(m_ref.shape, MASKED, F32)
            l_ref[...] = jnp.zeros_like(l_ref)
        acc_ref[...] = jnp.zeros_like(acc_ref)

    @pl.when(kj <= kv_last)
    def _():
        k = k_ref[...].astype(BF16)
        v_t = v_ref[...].T.astype(BF16)
        k0 = kj * tk
        diff = lax.broadcasted_iota(jnp.int32, (tk, rc), 1) - lax.broadcasted_iota(jnp.int32, (tk, rc), 0)
        if mode == "sel":
            blk_of_key = jnp.right_shift(k0 + lax.broadcasted_iota(jnp.int32, (tk, LANES), 0),
                                         NSA_BLOCK_SHIFT)
            onehot = jnp.where(blk_of_key == lax.broadcasted_iota(jnp.int32, (tk, LANES), 1), 1.0, 0.0)
            k = jnp.concatenate([k, onehot.astype(BF16)], axis=1)
            penalty = jnp.where(sel_ref[...] > 0.5, 0.0, MASKED).astype(BF16)
        if mode == "fox":
            key_bias = (fq_ref[:, 0:1] - _lane_pick(fk_ref[...], E_FORGET_LANE + g)) * LOG2E
        else:
            key_pos = (lax.broadcasted_iota(jnp.int32, (tk, 1), 0) + (k0 - q0)).astype(F32)

        def query(r, rows):
            if mode == "swa":
                qp = q_ref[rows, (r // 2) * LANES:(r // 2 + 1) * LANES]
                upper = lax.broadcasted_iota(jnp.int32, qp.shape, 1) >= SWA_HEAD_DIM
                qr = jnp.where(upper if r % 2 else ~upper, qp, 0.0)
            else:
                qr = q_ref[rows, r * HEAD_DIM:(r + 1) * HEAD_DIM]
            qr = (qr * qscale).astype(BF16)
            if mode == "sel":
                qr = jnp.concatenate([qr, penalty[rows]], axis=1)
            return qr

        for c in range(n_chunks):
            rows = slice(c * rc, (c + 1) * rc)
            shift = q0 + c * rc - k0
            mask = diff >= -shift
            if window is not None:
                mask = mask & (diff <= window - shift)
            q_all = jnp.concatenate([query(r, rows) for r in range(rep)], axis=0)
            s_all = lax.dot_general(k, q_all, (((1,), (1,)), ((), ())), preferred_element_type=F32)
            parts = []
            for r in range(rep):
                bias = key_bias if mode == "fox" else (slopes_ref[g * rep + r] * LOG2E) * key_pos
                parts.append(jnp.where(mask, s_all[:, head_lanes(r)] + bias, MASKED))
            s = jnp.concatenate(parts, axis=1)
            m_prev = m_ref[c]
            m_new = jnp.maximum(m_prev, jnp.max(s, axis=0, keepdims=True))
            alpha = jnp.exp2(m_prev - m_new)
            p = jnp.exp2(s - m_new)
            l_ref[c] = alpha * l_ref[c] + jnp.sum(p, axis=0, keepdims=True)
            acc_ref[c] = alpha * acc_ref[c] + jnp.dot(v_t, p.astype(BF16), preferred_element_type=F32)
            m_ref[c] = m_new

    @pl.when(j == nsteps - 1)
    def _():
        if mode in ("sel", "win"):
            gates = jax.nn.sigmoid(gz_ref[...] + gb_ref[...])
        for c in range(n_chunks):
            rows = slice(c * rc, (c + 1) * rc)
            o_t = acc_ref[c] / l_ref[c]
            out = lambda r: o_t[:, head_lanes(r)].T
            for r in range(rep):
                blk = slice(r * HEAD_DIM, (r + 1) * HEAD_DIM)
                if mode in ("sel", "win"):
                    gate = _lane_pick(gates[rows], branch * NSA_HEADS + g * rep + r)
                    o_ref[rows, blk] = prev_ref[rows, blk] + gate * out(r)
                elif mode == "fox":
                    o_ref[rows, :] = out(r)
                elif r % 2:
                    o = out(r)
                    upper = lax.broadcasted_iota(jnp.int32, o.shape, 1) >= SWA_HEAD_DIM
                    o_ref[rows, (r // 2) * LANES:(r // 2 + 1) * LANES] = jnp.where(upper, o, out(r - 1))


def _flash_call(mode, z, *, tq, tk, rep, n_kv, window, branch, q_col, k_col, v_col, out_cols,
                smem=(), extra=(), extra_specs=()):
    b, s, _ = z.shape
    tq, tk = _pick(s, tq), _pick(s, tk)
    nq = s // tq

    def first_last(qi):
        first = 0 if window is None else max(qi * tq - window, 0) // tk
        return first, (qi * tq + tq - 1) // tk

    nsteps = max(l - f + 1 for f, l in map(first_last, range(nq)))
    qw = rep * HEAD_DIM if mode != "swa" else rep * SWA_HEAD_DIM

    def kv_map(col):
        def index(bi, g, i, j):
            kj, last = _kv_tile(i, j, tq=tq, tk=tk, window=window)
            return bi, jnp.minimum(kj, last), col + g
        return index

    in_specs = [pl.BlockSpec(memory_space=pltpu.SMEM) for _ in smem]
    in_specs += [pl.BlockSpec((None, tq, qw), lambda bi, g, i, j: (bi, i, q_col + g)),
                 pl.BlockSpec((None, tk, LANES), kv_map(k_col)),
                 pl.BlockSpec((None, tk, LANES), kv_map(v_col))]
    in_specs += list(extra_specs(tq, tk, kv_map))
    rc = min(tq, max(LANES, LOGIT_CHUNK_VREGS * 8 * LANES // (tk * rep)))
    kern = functools.partial(_flash_t_kernel, mode=mode, tq=tq, tk=tk, rep=rep, nsteps=nsteps,
                             window=window, branch=branch, rc=rc)
    return pl.pallas_call(
        kern,
        out_shape=jax.ShapeDtypeStruct((b, s, out_cols), F32),
        grid=(b, n_kv, nq, nsteps),
        in_specs=in_specs,
        out_specs=pl.BlockSpec((None, tq, qw), lambda bi, g, i, j: (bi, i, g)),
        scratch_shapes=[pltpu.VMEM((tq // rc, 1, rep * rc), F32), pltpu.VMEM((tq // rc, 1, rep * rc), F32),
                        pltpu.VMEM((tq // rc, LANES, rep * rc), F32)],
        compiler_params=_cparams(("parallel", "parallel", "parallel", "arbitrary")),
        name="flash_" + mode,
    )(*smem, z, z, z, *extra)


def _nsa_gate_specs(tq):
    return [pl.BlockSpec((None, tq, NSA_REP * HEAD_DIM), lambda bi, g, i, j: (bi, i, g)),
            pl.BlockSpec((None, tq, LANES), lambda bi, g, i, j: (bi, i, E_GATE // LANES)),
            pl.BlockSpec((1, LANES), lambda bi, g, i, j: (0, 0))]


def nsa_selected(z, sel, prev, slopes, gate_b, *, t=512):
    def specs(tq, tk, kv_map):
        return [pl.BlockSpec((None, None, tq, LANES), lambda bi, g, i, j: (bi, g, i, 0))] + _nsa_gate_specs(tq)
    kvc = E_KVA // HEAD_DIM
    return _flash_call("sel", z, tq=t, tk=t, rep=NSA_REP, n_kv=NSA_KV, window=None, branch=1,
                       q_col=0, k_col=kvc + 2 * NSA_KV, v_col=kvc + 3 * NSA_KV,
                       out_cols=NSA_HEADS * HEAD_DIM, smem=(slopes,), extra=(sel, prev, z, gate_b),
                       extra_specs=specs)


def nsa_window(z, prev, slopes, gate_b, *, t=512):
    def specs(tq, tk, kv_map):
        return _nsa_gate_specs(tq)
    kvc = E_KVA // HEAD_DIM
    return _flash_call("win", z, tq=t, tk=t, rep=NSA_REP, n_kv=NSA_KV, window=NSA_WINDOW, branch=2,
                       q_col=0, k_col=kvc + 4 * NSA_KV, v_col=kvc + 5 * NSA_KV,
                       out_cols=NSA_HEADS * HEAD_DIM, smem=(slopes,), extra=(prev, z, gate_b),
                       extra_specs=specs)


def fox_attention(z, f_cum, f_cum_t, *, t=512):
    def specs(tq, tk, kv_map):
        def fk_map(bi, g, i, j):
            return (bi, kv_map(0)(bi, g, i, j)[1], 0)
        return [pl.BlockSpec((None, None, 1, tq), lambda bi, g, i, j: (bi, g, 0, i)),
                pl.BlockSpec((None, tk, LANES), fk_map)]
    return _flash_call("fox", z, tq=t, tk=2 * t, rep=1, n_kv=FOX_HEADS, window=None, branch=0,
                       q_col=E_QB // HEAD_DIM, k_col=E_KB // HEAD_DIM, v_col=E_VB // HEAD_DIM,
                       out_cols=FOX_HEADS * HEAD_DIM, extra=(f_cum_t, f_cum), extra_specs=specs)


def swa_attention(z, slopes, sink, *, tq=256, tk=128):
    return _flash_call("swa", z, tq=tq, tk=tk, rep=SWA_REP, n_kv=SWA_KV, window=SWA_WINDOW, branch=0,
                       q_col=0, k_col=O_K // LANES, v_col=O_V // LANES,
                       out_cols=SWA_HEADS * SWA_HEAD_DIM, smem=(slopes, sink),
                       extra_specs=lambda tq, tk, kv_map: [])


def _fox_gates_kernel(gz_ref, fb_ref, logf_ref, f_ref, ft_ref, *, chunk):
    x = gz_ref[...] + fb_ref[...]
    logf = jnp.minimum(x, 0.0) - jnp.log1p(jnp.exp(-jnp.abs(x)))
    logf_ref[...] = logf
    tri = jnp.where(lax.broadcasted_iota(jnp.int32, (chunk, chunk), 0)
                    >= lax.broadcasted_iota(jnp.int32, (chunk, chunk), 1), 1.0, 0.0).astype(BF16)
    carry = jnp.zeros((1, LANES), F32)
    for c in range(logf.shape[0] // chunk):
        rows = slice(c * chunk, (c + 1) * chunk)
        y = carry
        for part in _split3(logf[rows]):
            y = y + jnp.dot(tri, part, preferred_element_type=F32)
        f_ref[rows, :] = y
        carry = y[chunk - 1:chunk, :]
    ft_ref[...] = f_ref[...].T


def fox_gates(z, forget_b_lanes, *, chunk=256):
    b, s, _ = z.shape
    chunk = _pick(s, chunk)
    shp = jax.ShapeDtypeStruct((b, s, LANES), F32)
    return pl.pallas_call(
        functools.partial(_fox_gates_kernel, chunk=chunk),
        out_shape=(shp, shp, jax.ShapeDtypeStruct((b, LANES, s), F32)),
        grid=(b,),
        in_specs=[pl.BlockSpec((None, s, LANES), lambda bi: (bi, 0, E_GATE // LANES)),
                  pl.BlockSpec((1, LANES), lambda bi: (0, 0))],
        out_specs=(pl.BlockSpec((None, s, LANES), lambda bi: (bi, 0, 0)),
                   pl.BlockSpec((None, s, LANES), lambda bi: (bi, 0, 0)),
                   pl.BlockSpec((None, LANES, s), lambda bi: (bi, 0, 0))),
        compiler_params=_cparams(("parallel",)),
        name="fox_gates",
    )(z, forget_b_lanes)


def _alibi_slopes(n):
    return 2.0 ** (-8.0 * jnp.arange(1, n + 1, dtype=F32) / n)


def _prep_even_w_in(w):
    d = w.shape[0]
    n_gate = 3 * NSA_HEADS
    src_gate = E_QB
    src_qb = src_gate + n_gate
    src_forget = src_qb + 3 * FOX_HEADS * HEAD_DIM
    used = E_GATE + n_gate + FOX_HEADS
    return jnp.concatenate([w[:, :src_gate], w[:, src_qb:src_forget], w[:, src_gate:src_qb],
                            w[:, src_forget:], jnp.zeros((d, E_COLS - used), w.dtype)], axis=1).astype(BF16)


def _prep_odd_w_in(w):
    d = w.shape[0]
    kw = SWA_KV * SWA_HEAD_DIM
    k = w[:, O_K:O_K + kw].reshape(d, SWA_KV, SWA_HEAD_DIM)
    v = w[:, O_K + kw:].reshape(d, SWA_KV, SWA_HEAD_DIM)
    dup = lambda x: jnp.concatenate([x, x], axis=-1).reshape(d, SWA_KV * LANES)
    return jnp.concatenate([w[:, :O_K], dup(k), dup(v)], axis=1).astype(BF16)


def _lane_row(values, offset):
    return jnp.zeros((1, LANES), F32).at[0, offset:offset + values.shape[0]].set(values)


def _ffn_down_ple_kernel(a_ref, wd_ref, g3_ref, h_ref, p_ref, wpp_ref, wpg_ref, g4_ref, o_ref):
    y = jnp.dot(a_ref[...], wd_ref[...], preferred_element_type=F32)
    h = h_ref[...] + _rms(y, g3_ref[...])
    proj = jnp.dot(p_ref[...].astype(BF16), wpp_ref[...], preferred_element_type=F32)
    gate = jnp.dot(h.astype(BF16), wpg_ref[...], preferred_element_type=F32)
    o_ref[...] = h + _rms(proj, g4_ref[...]) * jax.nn.sigmoid(gate)


def ffn_down_ple(act, w_down, g3, h, p, w_pp, w_pg, g4, *, tm=256):
    m, d = h.shape
    f = act.shape[1]
    pd = p.shape[1]
    tm = _pick(m, tm)
    row = lambda width: pl.BlockSpec((tm, width), lambda i: (i, 0))
    whole = lambda shape: pl.BlockSpec(shape, lambda i: (0, 0))
    return pl.pallas_call(
        _ffn_down_ple_kernel,
        out_shape=jax.ShapeDtypeStruct((m, d), F32),
        grid=(m // tm,),
        in_specs=[row(f), whole((f, d)), whole((1, d)), row(d), row(pd), whole((pd, d)), whole((d, d)),
                  whole((1, d))],
        out_specs=row(d),
        compiler_params=_cparams(("parallel",)),
        name="ffn_down_ple",
    )(act, w_down, g3.reshape(1, d), h, p, w_pp, w_pg, g4.reshape(1, d))


def _ffn_and_ple(h, p, g, layer, w_up, w_down, w_pp, w_pg):
    act = prenorm_swiglu(h, g[2], w_up, layer)
    return ffn_down_ple(act, w_down, g[3], h, p, w_pp, w_pg, g[4])


def _even_prompt(h, b, s, g, w_in, w_out, gate_b, forget_b, pe, phi, slopes):
    z = prenorm_matmul(h, g[0], w_in).reshape(b, s, E_COLS)
    o_a, sel = nsa_compress_select(z, slopes, pe, phi, gate_b)
    o_a = nsa_selected(z, sel, o_a, slopes, gate_b)
    o_a = nsa_window(z, o_a, slopes, gate_b)
    logf, f_cum, f_cum_t = fox_gates(z, forget_b)
    fl = slice(E_FORGET_LANE, E_FORGET_LANE + FOX_HEADS)
    o_b = fox_attention(z, f_cum, f_cum_t[:, fl, :].reshape(b, FOX_HEADS, 1, s))
    m = b * s
    h = concat_matmul_postnorm_residual([o_a.reshape(m, -1), o_b.reshape(m, -1)], w_out, g[1], h)
    wb = min(NSA_WINDOW, s)
    win_off = E_KVA + 4 * NSA_KV * HEAD_DIM
    outs = (z[:, :, E_KVA:win_off].reshape(b, s, 4, NSA_KV, HEAD_DIM),
            z[:, :, E_KB:E_GATE].reshape(b, s, 2, FOX_HEADS, HEAD_DIM),
            logf[:, :, fl],
            z[:, s - wb:, win_off:E_QB].reshape(b, wb, 2, NSA_KV, HEAD_DIM))
    return h, outs


def _odd_prompt(h, b, s, g, w_in, w_out, sink, slopes):
    z = prenorm_matmul(h, g[0], w_in).reshape(b, s, O_COLS)
    o = swa_attention(z, slopes, sink)
    h = concat_matmul_postnorm_residual([o.reshape(b * s, -1)], w_out, g[1], h)
    wb = min(SWA_WINDOW, s)
    tail = z[:, s - wb:, O_K:].reshape(b, wb, 2, SWA_KV, 2, SWA_HEAD_DIM)[:, :, :, :, 0]
    return h, tail


NSA_POOL_PAGES = 16
FOX_PAGES = 16


def _nsa_pool_kernel(pt_ref, *refs, n_pages):
    page_refs, (pe_ref, o_ref) = refs[:n_pages], refs[n_pages:]
    blocks_per_page = PAGE_SIZE // NSA_BLOCK
    for i, ref in enumerate(page_refs):
        x = ref[...].reshape((blocks_per_page, NSA_BLOCK) + ref.shape[1:]) + pe_ref[...][None]
        o_ref[i * blocks_per_page:(i + 1) * blocks_per_page] = jnp.mean(x, axis=1)


def nsa_pool_pages(cache, page_table, pe_slabs):
    db, n_pages = page_table.shape
    slabs = cache.shape[2:]
    p = _pick(n_pages, NSA_POOL_PAGES)
    bpp = PAGE_SIZE // NSA_BLOCK

    def page_map(i):
        return lambda b, c, pt: (pt[b, c * p + i], 0, 0, 0)

    return pl.pallas_call(
        functools.partial(_nsa_pool_kernel, n_pages=p),
        out_shape=jax.ShapeDtypeStruct((db, n_pages * bpp) + slabs, F32),
        grid_spec=pltpu.PrefetchScalarGridSpec(
            num_scalar_prefetch=1,
            grid=(db, n_pages // p),
            in_specs=[pl.BlockSpec((PAGE_SIZE, None) + slabs, page_map(i)) for i in range(p)]
            + [pl.BlockSpec((NSA_BLOCK,) + slabs, lambda b, c, pt: (0, 0, 0))],
            out_specs=pl.BlockSpec((None, p * bpp) + slabs, lambda b, c, pt: (b, c, 0, 0))),
        compiler_params=_cparams(("parallel", "arbitrary")),
        name="nsa_pool_pages",
    )(page_table, *([cache] * p), pe_slabs)


def _transposed_rows(row):
    n = row.shape[1]
    bc = jnp.broadcast_to(row, (n, n))
    return bc.T, bc


def _nsa_sample_cmp_kernel(q_ref, pooled_ref, phi_ref, slopes_ref, gz_ref, gb_ref, o_ref, idx_ref,
                           *, q_pos, n_pick):
    n_slabs = 2 * NSA_KV
    npb = pooled_ref.shape[0] // n_slabs
    q = q_ref[...].astype(BF16)
    slab = lambda s: pooled_ref[pl.ds(s, npb, stride=n_slabs), :].astype(BF16)
    n_row = lax.broadcasted_iota(jnp.int32, (1, npb), 1)
    complete = (n_row + 1) * NSA_BLOCK - 1 <= q_pos
    dist = q_pos - (n_row.astype(F32) * NSA_BLOCK + (NSA_BLOCK - 1) / 2)
    head_row = lax.broadcasted_iota(jnp.int32, (NSA_HEADS, 1), 0)
    gates = jax.nn.sigmoid(gz_ref[...] + gb_ref[...])
    lane = lax.broadcasted_iota(jnp.int32, (NSA_HEADS, LANES), 1)
    gate_cmp = jnp.sum(jnp.where(lane == head_row, gates, 0.0), axis=-1, keepdims=True)
    scale = HEAD_DIM ** -0.5
    i_idx = lax.broadcasted_iota(jnp.int32, (npb, npb), 0)
    j_idx = lax.broadcasted_iota(jnp.int32, (npb, npb), 1)
    o_acc = jnp.zeros((NSA_HEADS, HEAD_DIM), F32)
    for g in range(NSA_KV):
        in_group = (head_row >= g * NSA_REP) & (head_row < (g + 1) * NSA_REP)
        kc = jnp.dot(slab(g), phi_ref[0].astype(BF16), preferred_element_type=F32).astype(BF16)
        vc = jnp.dot(slab(NSA_KV + g), phi_ref[1].astype(BF16), preferred_element_type=F32).astype(BF16)
        s = lax.dot_general(q, kc, (((1,), (1,)), ((), ())), preferred_element_type=F32) * scale
        s = s - slopes_ref[...] * dist
        s = jnp.where(complete, s, MASKED)
        m = jnp.max(s, axis=-1, keepdims=True)
        m = jnp.where(m > 0.5 * MASKED, m, 0.0)
        e = jnp.exp(s - m)
        p = e / jnp.maximum(jnp.sum(e, axis=-1, keepdims=True), 1e-30)
        o_g = jnp.dot(p.astype(BF16), vc, preferred_element_type=F32)
        o_acc = jnp.where(in_group, o_g, o_acc)
        imp = jnp.sum(jnp.where(in_group, p, 0.0), axis=0, keepdims=True)

        cur = q_pos // NSA_BLOCK
        forced = (n_row == 0) | (n_row == cur) | (n_row == cur - 1)
        score = jnp.where(forced, NSA_FORCE, jnp.where(complete, imp, -NSA_FORCE))
        col, bc = _transposed_rows(score)
        rank = jnp.sum(jnp.where((col > bc) | ((col == bc) & (i_idx < j_idx)), 1.0, 0.0),
                       axis=0, keepdims=True)
        sel = jnp.where(rank < n_pick, 1.0, 0.0)
        sel_col, _ = _transposed_rows(sel)
        pos = jnp.sum(jnp.where(i_idx < j_idx, sel_col, 0.0), axis=0, keepdims=True)
        slot = lax.broadcasted_iota(jnp.int32, (NSA_N_SEL, npb), 0).astype(F32)
        n_f = lax.broadcasted_iota(jnp.int32, (NSA_N_SEL, npb), 1).astype(F32)
        hit = (sel > 0.5) & (pos == slot)
        ids = jnp.sum(jnp.where(hit, n_f, 0.0), axis=-1, keepdims=True)
        idx_ref[g] = jnp.broadcast_to(ids, (NSA_N_SEL, LANES)).astype(jnp.int32)
    o_ref[...] = gate_cmp * o_acc


def nsa_sample_compress_select(q, pooled, phi, slopes, gz, gate_b, *, q_pos, n_pick):
    db = q.shape[0]
    pooled = pooled.reshape(db, -1, HEAD_DIM)
    npb = pooled.shape[1]
    kern = functools.partial(_nsa_sample_cmp_kernel, q_pos=q_pos, n_pick=n_pick)
    return pl.pallas_call(
        kern,
        out_shape=(jax.ShapeDtypeStruct((db, NSA_HEADS, HEAD_DIM), F32),
                   jax.ShapeDtypeStruct((db, NSA_KV, NSA_N_SEL, LANES), jnp.int32)),
        grid=(db,),
        in_specs=[pl.BlockSpec((None, NSA_HEADS, HEAD_DIM), lambda b: (b, 0, 0)),
                  pl.BlockSpec((None, npb, pooled.shape[2]), lambda b: (b, 0, 0)),
                  pl.BlockSpec(phi.shape, lambda b: (0, 0, 0)),
                  pl.BlockSpec((NSA_HEADS, 1), lambda b: (0, 0)),
                  pl.BlockSpec((None, 1, LANES), lambda b: (b, 0, 0)),
                  pl.BlockSpec((1, LANES), lambda b: (0, 0))],
        out_specs=(pl.BlockSpec((None, NSA_HEADS, HEAD_DIM), lambda b: (b, 0, 0)),
                   pl.BlockSpec((None, NSA_KV, NSA_N_SEL, LANES), lambda b: (b, 0, 0, 0))),
        compiler_params=_cparams(("parallel",)),
        name="nsa_sample_compress_select",
    )(q, pooled, phi, slopes.reshape(NSA_HEADS, 1), gz, gate_b)


def _nsa_sample_attend_kernel(idx_ref, pt_ref, *refs, n_pick, q_pos):
    n_blk = NSA_KV * n_pick
    n_slabs = 4 * NSA_KV
    blk_refs = refs[:n_blk]
    q_ref, new_ref, win_ref, prev_ref, slopes_ref, gz_ref, gb_ref, o_ref, win_out_ref = refs[n_blk:]
    b = pl.program_id(0)
    n_win_slabs = 2 * NSA_KV
    w = win_ref.shape[0] // n_win_slabs
    scale = HEAD_DIM ** -0.5
    gates = jax.nn.sigmoid(gz_ref[...] + gb_ref[...])
    r_col = lax.broadcasted_iota(jnp.int32, (NSA_REP, 1), 0)
    lane = lax.broadcasted_iota(jnp.int32, (NSA_REP, LANES), 1)
    row_w = lax.broadcasted_iota(jnp.int32, (w, HEAD_DIM), 0)
    n_keys = n_pick * NSA_BLOCK
    key_lane = lax.broadcasted_iota(jnp.int32, (1, n_keys), 1)

    def rounded(x):
        return x.astype(BF16).astype(F32)

    def attend(q4, keys, vals, k_new, v_new, dist, valid, slope):
        s = lax.dot_general(q4.astype(BF16), keys.astype(BF16), (((1,), (1,)), ((), ())),
                            preferred_element_type=F32) * scale - slope * dist
        s = jnp.where(valid, s, MASKED)
        s_new = jnp.sum(rounded(q4) * rounded(k_new), axis=-1, keepdims=True) * scale
        m = jnp.maximum(jnp.max(s, axis=-1, keepdims=True), s_new)
        e = jnp.exp(s - m)
        e_new = jnp.exp(s_new - m)
        den = jnp.sum(e, axis=-1, keepdims=True) + e_new
        num = jnp.dot(e.astype(BF16), vals.astype(BF16), preferred_element_type=F32)
        return (num + rounded(e_new) * rounded(v_new)) / den

    for g in range(NSA_KV):
        q4 = q_ref[g]
        slope = slopes_ref[g]
        new = lambda c: new_ref[c * NSA_KV + g:c * NSA_KV + g + 1, :]
        gate = lambda c: jnp.sum(jnp.where(lane == c * NSA_HEADS + g * NSA_REP + r_col, gates, 0.0),
                                 axis=-1, keepdims=True)
        blocks = blk_refs[g * n_pick:(g + 1) * n_pick]
        slab = lambda r, c: r[pl.ds(c * NSA_KV + g, NSA_BLOCK, stride=n_slabs), :]
        keys = jnp.concatenate([slab(r, 2) for r in blocks], axis=0)
        vals = jnp.concatenate([slab(r, 3) for r in blocks], axis=0)
        k_pos = key_lane & (NSA_BLOCK - 1)
        for i in range(n_pick):
            in_blk = jnp.right_shift(key_lane, NSA_BLOCK_SHIFT) == i
            k_pos = k_pos + jnp.where(in_blk, idx_ref[b, g, i] * NSA_BLOCK, 0)
        dist = q_pos - k_pos
        o_sel = attend(q4, keys, vals, new(2), new(3), dist.astype(F32), dist >= 0, slope)
        win_slab = lambda c: pl.ds(c * NSA_KV + g, w, stride=n_win_slabs)
        k_w = win_ref[win_slab(0), :]
        v_w = win_ref[win_slab(1), :]
        dist_w = w - lax.broadcasted_iota(jnp.int32, (1, w), 1)
        o_win = attend(q4, k_w, v_w, new(4), new(5), dist_w.astype(F32), dist_w <= NSA_WINDOW, slope)
        o_ref[g] = prev_ref[g] + gate(1) * o_sel + gate(2) * o_win
        for c, buf in ((0, k_w), (1, v_w)):
            rolled = pltpu.roll(buf, shift=w - 1, axis=0)
            win_out_ref[win_slab(c), :] = jnp.where(row_w == w - 1, new(4 + c), rolled)


def nsa_sample_attend(cache, idx, page_table, q, new_rows, win, prev, slopes, gz, gate_b, *, n_pick, q_pos):
    db = q.shape[0]
    bpp = PAGE_SIZE // NSA_BLOCK
    bpp_shift = bpp.bit_length() - 1

    def blk_map(g, i):
        def index(b, idx_r, pt_r):
            blk = idx_r[b, g, i]
            return pt_r[b, jnp.right_shift(blk, bpp_shift)] * bpp + (blk & (bpp - 1)), 0
        return index

    blk_specs = [pl.BlockSpec((NSA_BLOCK * 4 * NSA_KV, HEAD_DIM), blk_map(g, i))
                 for g in range(NSA_KV) for i in range(n_pick)]
    full = lambda shape: pl.BlockSpec(shape, lambda b, idx_r, pt_r: (0,) * len(shape))
    per_b = lambda shape: pl.BlockSpec((None,) + shape, lambda b, idx_r, pt_r: (b,) + (0,) * len(shape))
    kern = functools.partial(_nsa_sample_attend_kernel, n_pick=n_pick, q_pos=q_pos)
    return pl.pallas_call(
        kern,
        out_shape=(jax.ShapeDtypeStruct((db, NSA_KV, NSA_REP, HEAD_DIM), F32),
                   jax.ShapeDtypeStruct(win.shape, F32)),
        grid_spec=pltpu.PrefetchScalarGridSpec(
            num_scalar_prefetch=2,
            grid=(db,),
            in_specs=blk_specs + [per_b((NSA_KV, NSA_REP, HEAD_DIM)), per_b(new_rows.shape[1:]),
                                  per_b(win.shape[1:]), per_b((NSA_KV, NSA_REP, HEAD_DIM)),
                                  full((NSA_KV, NSA_REP, 1)), per_b((1, LANES)), full((1, LANES))],
            out_specs=(per_b((NSA_KV, NSA_REP, HEAD_DIM)), per_b(win.shape[1:]))),
        compiler_params=_cparams(("arbitrary",)),
        name="nsa_sample_attend",
    )(idx, page_table, *([cache] * len(blk_specs)), q, new_rows, win, prev,
      slopes.reshape(NSA_KV, NSA_REP, 1), gz, gate_b)


def _fox_decode_kernel(pt_ref, *refs, n_pages):
    kv_refs, lf_refs = refs[:n_pages], refs[n_pages:2 * n_pages]
    (q_ref, knew_ref, vnew_ref, vrow_ref, fraw_ref, fb_ref,
     o_ref, lfnew_ref, qbd_ref, m_ref, l_ref, acc_ref, carry_ref) = refs[2 * n_pages:]
    c = pl.program_id(1)
    width = FOX_HEADS * HEAD_DIM
    scale = HEAD_DIM ** -0.5
    head_row = lax.broadcasted_iota(jnp.int32, (FOX_HEADS, width), 0)
    head_of_lane = jnp.right_shift(lax.broadcasted_iota(jnp.int32, (FOX_HEADS, width), 1),
                                   HEAD_DIM.bit_length() - 1)

    @pl.when(c == 0)
    def _():
        q = q_ref[...]
        qbd_ref[...] = jnp.where(head_of_lane == head_row, jnp.concatenate([q] * FOX_HEADS, axis=1),
                                 0.0).astype(BF16)
        x = fraw_ref[...] + fb_ref[...]
        lf_new = jnp.minimum(x, 0.0) - jnp.log1p(jnp.exp(-jnp.abs(x)))
        lfnew_ref[...] = lf_new
        carry_ref[...] = lf_new
        rounded = lambda t: t.astype(BF16).astype(F32)
        m_ref[...] = jnp.sum(rounded(q) * rounded(knew_ref[...]), axis=-1, keepdims=True) * scale
        l_ref[...] = jnp.ones_like(l_ref)
        acc_ref[...] = jnp.broadcast_to(rounded(vrow_ref[...]), acc_ref.shape)

    later = jnp.where(lax.broadcasted_iota(jnp.int32, (PAGE_SIZE, PAGE_SIZE), 0)
                      > lax.broadcasted_iota(jnp.int32, (PAGE_SIZE, PAGE_SIZE), 1), 1.0, 0.0).astype(BF16)
    lf_all = jnp.concatenate([lf_ref[...] for lf_ref in lf_refs], axis=0)
    rows = lf_all.shape[0]
    suffix3 = jnp.dot(jnp.concatenate(_split3(lf_all), axis=0), later, preferred_element_type=F32)
    suffix = suffix3[:rows] + suffix3[rows:2 * rows] + suffix3[2 * rows:]
    totals = jnp.sum(lf_all, axis=-1, keepdims=True)
    carry = carry_ref[...]
    biases = []
    for i in range(n_pages):
        page_rows = slice(i * FOX_HEADS, (i + 1) * FOX_HEADS)
        biases.append(carry + suffix[page_rows])
        carry = carry + totals[page_rows]
    carry_ref[...] = carry
    bias = jnp.concatenate(biases, axis=1)
    def heads_on_lanes(kv_ref, first):
        return jnp.concatenate([kv_ref[pl.ds(first + h, PAGE_SIZE, stride=2 * FOX_HEADS), :].astype(BF16)
                                for h in range(FOX_HEADS)], axis=1)

    kp = jnp.concatenate([heads_on_lanes(kv_ref, 0) for kv_ref in kv_refs], axis=0)
    vp = jnp.concatenate([heads_on_lanes(kv_ref, FOX_HEADS) for kv_ref in kv_refs], axis=0)
    s = lax.dot_general(qbd_ref[...], kp, (((1,), (1,)), ((), ())),
                        preferred_element_type=F32) * scale + bias
    m_prev = m_ref[...]
    m_new = jnp.maximum(m_prev, jnp.max(s, axis=-1, keepdims=True))
    alpha = jnp.exp(m_prev - m_new)
    p = jnp.exp(s - m_new)
    l_ref[...] = alpha * l_ref[...] + jnp.sum(p, axis=-1, keepdims=True)
    acc_ref[...] = alpha * acc_ref[...] + jnp.dot(p.astype(BF16), vp, preferred_element_type=F32)
    m_ref[...] = m_new

    @pl.when(c == pl.num_programs(1) - 1)
    def _():
        acc = jnp.where(head_of_lane == head_row, acc_ref[...], 0.0)
        o = acc[:, :HEAD_DIM]
        for h in range(1, FOX_HEADS):
            o = o + acc[:, h * HEAD_DIM:(h + 1) * HEAD_DIM]
        o_ref[...] = o / l_ref[...]


def fox_decode(cache_kv, cache_logf_t, page_table, q, k_new, v_new, f_raw, forget_b):
    db, n_pages = page_table.shape
    p = _pick(n_pages, FOX_PAGES)
    width = FOX_HEADS * HEAD_DIM
    page_rows = PAGE_SIZE * 2 * FOX_HEADS

    def page_map(i):
        return lambda b, c, pt: (pt[b, n_pages - 1 - (c * p + i)], 0, 0)

    def kv_page_map(i):
        return lambda b, c, pt: (pt[b, n_pages - 1 - (c * p + i)], 0)

    per_b = lambda shape: pl.BlockSpec((None,) + shape, lambda b, c, pt: (b,) + (0,) * len(shape))
    return pl.pallas_call(
        functools.partial(_fox_decode_kernel, n_pages=p),
        out_shape=(jax.ShapeDtypeStruct((db, FOX_HEADS, HEAD_DIM), F32),
                   jax.ShapeDtypeStruct((db, FOX_HEADS, 1), F32)),
        grid_spec=pltpu.PrefetchScalarGridSpec(
            num_scalar_prefetch=1,
            grid=(db, n_pages // p),
            in_specs=[pl.BlockSpec((page_rows, HEAD_DIM), kv_page_map(i)) for i in range(p)]
            + [pl.BlockSpec((None, FOX_HEADS, PAGE_SIZE), page_map(i)) for i in range(p)]
            + [per_b((FOX_HEADS, HEAD_DIM)), per_b((FOX_HEADS, HEAD_DIM)), per_b((FOX_HEADS, HEAD_DIM)),
               per_b((1, width)), per_b((FOX_HEADS, 1)),
               pl.BlockSpec((FOX_HEADS, 1), lambda b, c, pt: (0, 0))],
            out_specs=(per_b((FOX_HEADS, HEAD_DIM)), per_b((FOX_HEADS, 1))),
            scratch_shapes=[pltpu.VMEM((FOX_HEADS, width), BF16), pltpu.VMEM((FOX_HEADS, 1), F32),
                            pltpu.VMEM((FOX_HEADS, 1), F32), pltpu.VMEM((FOX_HEADS, width), F32),
                            pltpu.VMEM((FOX_HEADS, 1), F32)]),
        compiler_params=_cparams(("parallel", "arbitrary")),
        name="fox_decode",
    )(page_table, *([cache_kv] * p), *([cache_logf_t] * p), q, k_new, v_new,
      v_new.reshape(db, 1, width), f_raw, forget_b.reshape(FOX_HEADS, 1))


def _swa_decode_kernel(q_ref, new_ref, win_ref, slopes_ref, sink_ref, o_ref, win_out_ref):
    w = win_ref.shape[0]
    kvw = SWA_KV * SWA_HEAD_DIM
    scale = SWA_HEAD_DIM ** -0.5
    rounded = lambda t: t.astype(BF16).astype(F32)
    dist = (w - lax.broadcasted_iota(jnp.int32, (1, w), 1))
    new = new_ref[...]
    for g in range(SWA_KV):
        q = q_ref[g]
        lo = g * SWA_HEAD_DIM
        k_w = win_ref[:, lo:lo + SWA_HEAD_DIM]
        v_w = win_ref[:, kvw + lo:kvw + lo + SWA_HEAD_DIM]
        k_new = new[:, lo:lo + SWA_HEAD_DIM]
        v_new = new[:, kvw + lo:kvw + lo + SWA_HEAD_DIM]
        s = lax.dot_general(q.astype(BF16), k_w.astype(BF16), (((1,), (1,)), ((), ())),
                            preferred_element_type=F32) * scale - slopes_ref[g] * dist.astype(F32)
        s = jnp.where(dist <= SWA_WINDOW, s, MASKED)
        s_new = jnp.sum(rounded(q) * rounded(k_new), axis=-1, keepdims=True) * scale
        sink = sink_ref[g]
        m = jnp.maximum(jnp.maximum(jnp.max(s, axis=-1, keepdims=True), s_new), sink)
        e = jnp.exp(s - m)
        e_new = jnp.exp(s_new - m)
        den = jnp.sum(e, axis=-1, keepdims=True) + e_new + jnp.exp(sink - m)
        num = jnp.dot(e.astype(BF16), v_w.astype(BF16), preferred_element_type=F32)
        o_ref[g] = (num + rounded(e_new) * rounded(v_new)) / den
    buf = win_ref[...]
    row = lax.broadcasted_iota(jnp.int32, buf.shape, 0)
    win_out_ref[...] = jnp.where(row == w - 1, new, pltpu.roll(buf, shift=w - 1, axis=0))


def swa_decode(q, new_row, win, slopes, sink):
    db = q.shape[0]
    per_b = lambda shape: pl.BlockSpec((None,) + shape, lambda b: (b,) + (0,) * len(shape))
    full = lambda shape: pl.BlockSpec(shape, lambda b: (0,) * len(shape))
    return pl.pallas_call(
        _swa_decode_kernel,
        out_shape=(jax.ShapeDtypeStruct(q.shape, F32), jax.ShapeDtypeStruct(win.shape, F32)),
        grid=(db,),
        in_specs=[per_b(q.shape[1:]), per_b(new_row.shape[1:]), per_b(win.shape[1:]),
                  full((SWA_KV, SWA_REP, 1)), full((SWA_KV, SWA_REP, 1))],
        out_specs=(per_b(q.shape[1:]), per_b(win.shape[1:])),
        compiler_params=_cparams(("parallel",)),
        name="swa_decode",
    )(q, new_row, win, slopes.reshape(SWA_KV, SWA_REP, 1), sink.reshape(SWA_KV, SWA_REP, 1))


def _even_sample(h, g, w_in, w_out, gate_b, forget_b, pe, phi, slopes,
                 cache_nsa, cache_fox_kv, cache_fox_logf, win_buf, page_table, e):
    db = h.shape[0]
    past = page_table.shape[1] * PAGE_SIZE
    n_pick = NSA_N_SEL - 1
    assert past % NSA_BLOCK == 0 and past // NSA_BLOCK >= n_pick
    n_even, pool = cache_nsa.shape[:2]
    page_table = page_table + e * pool
    nsa_rows = cache_nsa.reshape(-1, HEAD_DIM)
    nsa_halves = cache_nsa.reshape(-1, 2, 2 * NSA_KV, HEAD_DIM)
    fox_rows = cache_fox_kv.reshape(-1, HEAD_DIM)
    logf_t = cache_fox_logf.reshape((n_even * pool,) + cache_fox_logf.shape[2:]).transpose(0, 2, 1)
    win_buf = win_buf.reshape((n_even * db,) + win_buf.shape[2:])[e * db:(e + 1) * db]
    z = prenorm_matmul(h, g[0], w_in)
    heads = lambda lo, hi: z[:, lo:hi].reshape(db, -1, HEAD_DIM)
    gz = z[:, E_GATE:E_GATE + LANES].reshape(db, 1, LANES)
    q_a = heads(E_QA, E_KVA)
    pe_slabs = jnp.stack([pe[0]] * NSA_KV + [pe[1]] * NSA_KV, axis=1)
    pooled = nsa_pool_pages(nsa_halves, page_table, pe_slabs)
    o_cmp, idx = nsa_sample_compress_select(q_a, pooled, phi, slopes, gz, gate_b, q_pos=past, n_pick=n_pick)
    grp = lambda x: x.reshape(db, NSA_KV, NSA_REP, HEAD_DIM)
    o_a, new_win = nsa_sample_attend(nsa_rows, idx[..., 0], page_table, grp(q_a), heads(E_KVA, E_QB),
                                     win_buf.reshape(db, -1, HEAD_DIM), grp(o_cmp), slopes, gz, gate_b,
                                     n_pick=n_pick, q_pos=past)
    o_b, lf_new = fox_decode(fox_rows, logf_t,
                             page_table, heads(E_QB, E_KB), heads(E_KB, E_VB), heads(E_VB, E_GATE),
                             z[:, E_GATE + E_FORGET_LANE:E_GATE + E_FORGET_LANE + FOX_HEADS].reshape(db, -1, 1),
                             forget_b)
    h = concat_matmul_postnorm_residual([o_a.reshape(db, -1), o_b.reshape(db, -1)], w_out, g[1], h)
    win_off = E_KVA + 4 * NSA_KV * HEAD_DIM
    outs = (z[:, E_KVA:win_off].reshape(db, 1, 4, NSA_KV, HEAD_DIM),
            z[:, E_KB:E_GATE].reshape(db, 1, 2, FOX_HEADS, HEAD_DIM),
            lf_new.reshape(db, 1, FOX_HEADS),
            new_win.reshape(win_buf.shape))
    return h, outs


def _odd_sample(h, g, w_in, w_out, sink, slopes, win_buf):
    db = h.shape[0]
    w = win_buf.shape[1]
    z = prenorm_matmul(h, g[0], w_in)
    q = z[:, :O_K].reshape(db, SWA_KV, SWA_REP, SWA_HEAD_DIM)
    new_row = z[:, O_K:].reshape(db, 2, SWA_KV, 2, SWA_HEAD_DIM)[:, :, :, 0].reshape(db, 1, -1)
    o, new_win = swa_decode(q, new_row, win_buf.reshape(db, w, -1), slopes, sink)
    h = concat_matmul_postnorm_residual([o.reshape(db, -1)], w_out, g[1], h)
    return h, new_win.reshape(win_buf.shape)


def kernel(x_prompt, x_sample, cache_nsa_kv, cache_fox_kv, cache_fox_logf, state_nsa_win, state_swa_win,
           page_table, p_prompt, p_sample, w_in_even, w_out_even, nsa_gate_b, nsa_pe, nsa_phi, fox_forget_b,
           w_in_odd, w_out_odd, swa_sink, norm_g, w_ffn_up, w_ffn_down, w_ple_proj, w_ple_gate):
    b, s, d = x_prompt.shape
    db, dec_seq, _ = x_sample.shape
    assert dec_seq == 1, "the decode kernels handle one new token per sequence"
    depth = norm_g.shape[0]
    nsa_slopes = _alibi_slopes(NSA_HEADS)
    swa_slopes = _alibi_slopes(SWA_HEADS)
    hp = x_prompt.reshape(b * s, d)
    hs = x_sample.reshape(db, d)
    w_ffn_up_bf16 = w_ffn_up.astype(BF16)
    even_p, even_s, odd_p, odd_s = [], [], [], []
    for i in range(depth):
        g = norm_g[i]
        if i % 2 == 0:
            e = i // 2
            w_in = _prep_even_w_in(w_in_even[e])
            w_out = w_out_even[e].astype(BF16)
            gate_b = _lane_row(nsa_gate_b[e], 0)
            hp, outs = _even_prompt(hp, b, s, g, w_in, w_out, gate_b,
                                    _lane_row(fox_forget_b[e], E_FORGET_LANE), nsa_pe[e], nsa_phi[e], nsa_slopes)
            even_p.append(outs)
            hs, outs = _even_sample(hs, g, w_in, w_out, gate_b, fox_forget_b[e], nsa_pe[e], nsa_phi[e],
                                    nsa_slopes, cache_nsa_kv, cache_fox_kv, cache_fox_logf,
                                    state_nsa_win, page_table, e)
            even_s.append(outs)
        else:
            o = i // 2
            w_in = _prep_odd_w_in(w_in_odd[o])
            w_out = w_out_odd[o].astype(BF16)
            hp, tail = _odd_prompt(hp, b, s, g, w_in, w_out, swa_sink[o], swa_slopes)
            odd_p.append(tail)
            hs, new_win = _odd_sample(hs, g, w_in, w_out, swa_sink[o], swa_slopes, state_swa_win[o])
            odd_s.append(new_win)
        ffn_w = (w_ffn_up_bf16, w_ffn_down[i].astype(BF16),
                 w_ple_proj[i].astype(BF16), w_ple_gate[i].astype(BF16))
        hp = _ffn_and_ple(hp, p_prompt[i].reshape(b * s, -1), g, i, *ffn_w)
        hs = _ffn_and_ple(hs, p_sample[i].reshape(db, -1), g, i, *ffn_w)
    stack = lambda seq, k: jnp.stack([t[k] for t in seq])
    return (hp.reshape(b, s, d), hs.reshape(db, 1, d),
            stack(even_p, 0), stack(even_s, 0), stack(even_p, 1), stack(even_s, 1),
            stack(even_p, 2), stack(even_s, 2), stack(even_p, 3), stack(even_s, 3),
            jnp.stack(odd_p), jnp.stack(odd_s))
```
